```python
import math
import jax
import jax.numpy as jnp
from jax import lax
import numpy as np

D_MODEL = 1024
BATCH = 8
SEQ = 2048
DEPTH = 1

ATTN_HEADS = 8
HEAD_DIM = 64
ATTN_WIDTH = ATTN_HEADS * HEAD_DIM
IDX_HEADS = 8
IDX_DIM = 32
MAX_TOPK_KEYS = 256
Q_BLOCK = 64
SSD_HEADS = 8
SSD_HEAD_DIM = 64
SSD_WIDTH = SSD_HEADS * SSD_HEAD_DIM
SSD_GROUPS = 2
D_STATE = 128
CONV_WIDTH = 4
CHUNK = 128
MIX_WIDTH = ATTN_WIDTH + SSD_WIDTH
PEER_HEADS = 8
N_KEYS = 128
N_EXPERTS = N_KEYS * N_KEYS
PEER_QUERY_DIM = 256
PEER_HALF = PEER_QUERY_DIM // 2
PEER_SUB_TOPK = 16
PEER_TOPK = 16
TOK_BLOCK = 128
ROPE_THETA = 10000.0
LN_EPS = 1e-5
DEEPNORM_ALPHA = (2 * DEPTH) ** 0.25
DEEPNORM_BETA = (8 * DEPTH) ** -0.25

IN_SIZES = (ATTN_WIDTH, ATTN_WIDTH, ATTN_WIDTH, IDX_HEADS * IDX_DIM, IDX_DIM, IDX_HEADS,
            SSD_WIDTH, SSD_WIDTH + 2 * SSD_GROUPS * D_STATE, SSD_HEADS)
IN_WIDTH = sum(IN_SIZES)
SPLIT_POINTS = tuple(sum(IN_SIZES[:i + 1]) for i in range(len(IN_SIZES) - 1))
XBC_WIDTH = SSD_WIDTH + 2 * SSD_GROUPS * D_STATE

kernel_name = "hymba_dsa_ssd_peer_deepnorm"


def layer_norm(x, g, b):
    xf = x.astype(jnp.float32)
    mu = jnp.mean(xf, -1, keepdims=True)
    var = jnp.mean(jnp.square(xf - mu), -1, keepdims=True)
    y = (xf - mu) * lax.rsqrt(var + LN_EPS)
    return (y * g.astype(jnp.float32) + b.astype(jnp.float32)).astype(x.dtype)


def rope(x, positions):
    d = x.shape[-1]
    inv = ROPE_THETA ** (-jnp.arange(0, d, 2, dtype=jnp.float32) / d)
    ang = positions.astype(jnp.float32)[..., None] * inv
    ang = ang.reshape(ang.shape[:2] + (1,) * (x.ndim - 3) + ang.shape[-1:])
    cos = jnp.cos(ang).astype(x.dtype)
    sin = jnp.sin(ang).astype(x.dtype)
    x1, x2 = x[..., : d // 2], x[..., d // 2:]
    return jnp.concatenate([x1 * cos - x2 * sin, x2 * cos + x1 * sin], axis=-1)


def dsa_attention(q, k, v, iq, ik, iw, positions):
    bsz, seq, _ = q.shape
    q = rope(q.reshape(bsz, seq, ATTN_HEADS, HEAD_DIM), positions)
    k = rope(k.reshape(bsz, seq, ATTN_HEADS, HEAD_DIM), positions)
    v = v.reshape(bsz, seq, ATTN_HEADS, HEAD_DIM)
    iq = rope(iq.reshape(bsz, seq, IDX_HEADS, IDX_DIM), positions)
    ik = rope(ik, positions)
    iw = iw * (IDX_HEADS ** -0.5)
    topk = min(MAX_TOPK_KEYS, seq // 4)
    nb = seq // Q_BLOCK
    key_pos = jnp.arange(seq)

    def blockify(a):
        return a.reshape((bsz, nb, Q_BLOCK) + a.shape[2:]).swapaxes(0, 1)

    def one_block(args):
        qb, iqb, iwb, start = args
        t = start + jnp.arange(Q_BLOCK)
        rel = jax.nn.relu(jnp.einsum('bqhd,bsd->bqhs', iqb, ik) * (IDX_DIM ** -0.5))
        score = jnp.einsum('bqh,bqhs->bqs', iwb, rel).astype(jnp.float32)
        causal = key_pos[None, :] <= t[:, None]
        score = jnp.where(causal[None], score, -jnp.inf)
        _, idx = lax.top_k(score, topk)
        kg = jax.vmap(lambda kb, ib: kb[ib])(k, idx)
        vg = jax.vmap(lambda vb, ib: vb[ib])(v, idx)
        logits = jnp.einsum('bqhd,bqkhd->bqhk', qb, kg).astype(jnp.float32) * (HEAD_DIM ** -0.5)
        valid = idx <= t[None, :, None]
        logits = jnp.where(valid[:, :, None, :], logits, -jnp.inf)
        p = jax.nn.softmax(logits, axis=-1).astype(vg.dtype)
        return jnp.einsum('bqhk,bqkhd->bqhd', p, vg)

    starts = jnp.arange(nb, dtype=jnp.int32) * Q_BLOCK
    out = lax.map(one_block, (blockify(q), blockify(iq), blockify(iw), starts))
    return out.swapaxes(0, 1).reshape(bsz, seq, ATTN_WIDTH)


def segsum(a):
    t = a.shape[-1]
    ac = jnp.cumsum(a, axis=-1)
    seg = ac[..., :, None] - ac[..., None, :]
    mask = jnp.tril(jnp.ones((t, t), dtype=bool))
    return jnp.where(mask, seg, -jnp.inf)


def ssd_scan(xs, dt, a, bm, cm):
    bsz, seq, nh, hp = xs.shape
    nc = seq // CHUNK
    rep = nh // bm.shape[2]
    bm = jnp.repeat(bm, rep, axis=2)
    cm = jnp.repeat(cm, rep, axis=2)
    n = bm.shape[-1]
    xc = (xs * dt[..., None]).reshape(bsz, nc, CHUNK, nh, hp)
    bc = bm.reshape(bsz, nc, CHUNK, nh, n)
    cc = cm.reshape(bsz, nc, CHUNK, nh, n)
    ac = (dt * a).reshape(bsz, nc, CHUNK, nh).transpose(0, 3, 1, 2)
    a_cum = jnp.cumsum(ac, axis=-1)
    decay_in = jnp.exp(segsum(ac))
    cb = jnp.einsum('bclhn,bcshn->bhcls', cc, bc)
    y_diag = jnp.einsum('bhcls,bcshp->bclhp', cb * decay_in, xc)
    decay_to_end = jnp.exp(a_cum[..., -1:] - a_cum).transpose(0, 2, 3, 1)
    states = jnp.einsum('bclhn,bclhp->bchpn', bc, xc * decay_to_end[..., None])
    states = jnp.concatenate([jnp.zeros_like(states[:, :1]), states], axis=1)
    chunk_decay = jnp.exp(segsum(jnp.pad(a_cum[..., -1], ((0, 0), (0, 0), (1, 0)))))
    carried = jnp.einsum('bhzc,bchpn->bzhpn', chunk_decay, states)[:, :-1]
    y_off = jnp.einsum('bclhn,bchpn->bclhp', cc, carried) * jnp.exp(a_cum).transpose(0, 2, 3, 1)[..., None]
    return (y_diag + y_off).reshape(bsz, seq, nh, hp)


def ssd_mixer(z, xbc, dt, conv_w, conv_b, dt_bias, a_log, d_skip, norm_w):
    bsz, seq, _ = z.shape
    xbc = lax.conv_general_dilated(
        xbc, conv_w[:, None, :].astype(xbc.dtype), window_strides=(1,),
        padding=[(CONV_WIDTH - 1, 0)], dimension_numbers=('NWC', 'WIO', 'NWC'),
        feature_group_count=XBC_WIDTH) + conv_b.astype(xbc.dtype)
    xbc = jax.nn.silu(xbc).astype(jnp.float32)
    xs = xbc[..., :SSD_WIDTH].reshape(bsz, seq, SSD_HEADS, SSD_HEAD_DIM)
    bm = xbc[..., SSD_WIDTH:SSD_WIDTH + SSD_GROUPS * D_STATE].reshape(bsz, seq, SSD_GROUPS, D_STATE)
    cm = xbc[..., SSD_WIDTH + SSD_GROUPS * D_STATE:].reshape(bsz, seq, SSD_GROUPS, D_STATE)
    dt = jax.nn.softplus(dt.astype(jnp.float32) + dt_bias.astype(jnp.float32))
    a = -jnp.exp(a_log.astype(jnp.float32))
    y = ssd_scan(xs, dt, a, bm, cm) + xs * d_skip.astype(jnp.float32)[:, None]
    y = y.reshape(bsz, seq, SSD_WIDTH) * jax.nn.silu(z.astype(jnp.float32))
    yg = y.reshape(bsz, seq, SSD_GROUPS, SSD_WIDTH // SSD_GROUPS)
    yg = yg * lax.rsqrt(jnp.mean(jnp.square(yg), -1, keepdims=True) + LN_EPS)
    y = yg.reshape(bsz, seq, SSD_WIDTH) * norm_w.astype(jnp.float32)
    return y.astype(z.dtype)


def token_mix(x, positions, w_in, conv_w, conv_b, dt_bias, a_log, d_skip, norm_w, w_out):
    proj = x @ w_in
    q, k, v, iq, ik, iw, z, xbc, dt = jnp.split(proj, SPLIT_POINTS, axis=-1)
    attn = dsa_attention(q, k, v, iq, ik, iw, positions)
    ssd = ssd_mixer(z, xbc, dt, conv_w, conv_b, dt_bias, a_log, d_skip, norm_w)
    return jnp.concatenate([attn, ssd], axis=-1) @ w_out


def peer(h, wq, k1, k2, u, v):
    bsz, seq, dm = h.shape
    xt = h.reshape(-1, dm)
    ntok = xt.shape[0]
    q = (xt @ wq).reshape(ntok, PEER_HEADS, 2, PEER_HALF)
    s1 = jnp.einsum('nhd,kd->nhk', q[:, :, 0], k1).astype(jnp.float32)
    s2 = jnp.einsum('nhd,kd->nhk', q[:, :, 1], k2).astype(jnp.float32)
    v1, i1 = lax.top_k(s1, PEER_SUB_TOPK)
    v2, i2 = lax.top_k(s2, PEER_SUB_TOPK)
    cand = (v1[..., :, None] + v2[..., None, :]).reshape(ntok, PEER_HEADS, -1)
    cidx = (i1[..., :, None] * N_KEYS + i2[..., None, :]).reshape(ntok, PEER_HEADS, -1)
    best, pos = lax.top_k(cand, PEER_TOPK)
    eidx = jnp.take_along_axis(cidx, pos, axis=-1).reshape(ntok, PEER_HEADS * PEER_TOPK)
    gate = jax.nn.softmax(best, axis=-1).reshape(ntok, PEER_HEADS * PEER_TOPK).astype(h.dtype)
    nb = ntok // TOK_BLOCK

    def one_block(args):
        xb, ib, gb = args
        act = jax.nn.gelu(jnp.einsum('td,tkd->tk', xb, u[ib]), approximate=False) * gb
        return jnp.einsum('tk,tkd->td', act, v[ib])

    out = lax.map(one_block, (xt.reshape(nb, TOK_BLOCK, dm),
                              eidx.reshape(nb, TOK_BLOCK, -1),
                              gate.reshape(nb, TOK_BLOCK, -1)))
    return out.reshape(bsz, seq, dm)


def setup_inputs(seed: int = 0) -> dict:
    key = jax.random.key(seed)
    ks = jax.random.split(key, 20)
    f32 = jnp.float32
    x = jax.random.normal(ks[0], (BATCH, SEQ, D_MODEL), f32)
    positions = jnp.broadcast_to(jnp.arange(SEQ, dtype=jnp.int32), (BATCH, SEQ))
    col_scale = jnp.ones((IN_WIDTH,), f32).at[2 * ATTN_WIDTH:3 * ATTN_WIDTH].set(DEEPNORM_BETA)
    w_in = jax.random.normal(ks[1], (DEPTH, D_MODEL, IN_WIDTH), f32) * (D_MODEL ** -0.5) * col_scale
    conv_w = jax.random.normal(ks[2], (DEPTH, CONV_WIDTH, XBC_WIDTH), f32) * (CONV_WIDTH ** -0.5)
    conv_b = 0.02 * jax.random.normal(ks[3], (DEPTH, XBC_WIDTH), f32)
    dt0 = jnp.exp(jax.random.uniform(ks[4], (DEPTH, SSD_HEADS), f32, math.log(1e-3), math.log(1e-1)))
    dt_bias = dt0 + jnp.log(-jnp.expm1(-dt0))
    a_log = jnp.log(jax.random.uniform(ks[5], (DEPTH, SSD_HEADS), f32, 1.0, 16.0))
    d_skip = 1.0 + 0.1 * jax.random.normal(ks[6], (DEPTH, SSD_HEADS), f32)
    ssm_norm_w = 1.0 + 0.02 * jax.random.normal(ks[7], (DEPTH, SSD_WIDTH), f32)
    w_out = jax.random.normal(ks[8], (DEPTH, MIX_WIDTH, D_MODEL), f32) * (MIX_WIDTH ** -0.5) * DEEPNORM_BETA
    ln1_g = 1.0 + 0.02 * jax.random.normal(ks[9], (DEPTH, D_MODEL), f32)
    ln1_b = 0.02 * jax.random.normal(ks[10], (DEPTH, D_MODEL), f32)
    peer_wq = jax.random.normal(ks[11], (DEPTH, D_MODEL, PEER_HEADS * PEER_QUERY_DIM), f32) * (D_MODEL ** -0.5)
    peer_k1 = jax.random.normal(ks[12], (DEPTH, N_KEYS, PEER_HALF), f32) * (PEER_HALF ** -0.5)
    peer_k2 = jax.random.normal(ks[13], (DEPTH, N_KEYS, PEER_HALF), f32) * (PEER_HALF ** -0.5)
    peer_u = jax.random.normal(ks[14], (DEPTH, N_EXPERTS, D_MODEL), f32) * (D_MODEL ** -0.5)
    peer_v = jax.random.normal(ks[15], (DEPTH, N_EXPERTS, D_MODEL), f32) * DEEPNORM_BETA
    ln2_g = 1.0 + 0.02 * jax.random.normal(ks[16], (DEPTH, D_MODEL), f32)
    ln2_b = 0.02 * jax.random.normal(ks[17], (DEPTH, D_MODEL), f32)
    return {"x": x, "positions": positions, "w_in": w_in, "conv_w": conv_w, "conv_b": conv_b,
            "dt_bias": dt_bias, "a_log": a_log, "d_skip": d_skip, "ssm_norm_w": ssm_norm_w,
            "w_out": w_out, "ln1_g": ln1_g, "ln1_b": ln1_b, "peer_wq": peer_wq,
            "peer_k1": peer_k1, "peer_k2": peer_k2, "peer_u": peer_u, "peer_v": peer_v,
            "ln2_g": ln2_g, "ln2_b": ln2_b}


def reference(x, positions, w_in, conv_w, conv_b, dt_bias, a_log, d_skip, ssm_norm_w,
              w_out, ln1_g, ln1_b, peer_wq, peer_k1, peer_k2, peer_u, peer_v, ln2_g, ln2_b):
    for i in range(DEPTH):
        mix = token_mix(x, positions, w_in[i], conv_w[i], conv_b[i], dt_bias[i], a_log[i],
                        d_skip[i], ssm_norm_w[i], w_out[i])
        x = layer_norm(DEEPNORM_ALPHA * x + mix, ln1_g[i], ln1_b[i])
        ffn = peer(x, peer_wq[i], peer_k1[i], peer_k2[i], peer_u[i], peer_v[i])
        x = layer_norm(DEEPNORM_ALPHA * x + ffn, ln2_g[i], ln2_b[i])
    return x
```

```python
import functools
import math

import jax
import jax.numpy as jnp
from jax import lax
from jax.experimental import pallas as pl
from jax.experimental.pallas import tpu as pltpu

F32 = jnp.float32
BF16 = jnp.bfloat16

ATTN_HEADS = 8
HEAD_DIM = 64
ATTN_WIDTH = ATTN_HEADS * HEAD_DIM
IDX_HEADS = 8
IDX_DIM = 32
MAX_TOPK_KEYS = 256
SSD_HEADS = 8
SSD_HEAD_DIM = 64
SSD_WIDTH = SSD_HEADS * SSD_HEAD_DIM
SSD_GROUPS = 2
D_STATE = 128
CONV_WIDTH = 4
CHUNK = 128
XBC_WIDTH = SSD_WIDTH + 2 * SSD_GROUPS * D_STATE
PEER_HEADS = 8
N_KEYS = 128
PEER_HALF = 128
PEER_SUB_TOPK = 16
PEER_TOPK = 16
ROPE_THETA = 10000.0
LN_EPS = 1e-5

MISC_IK = 0
MISC_IW = IDX_DIM
MISC_DT = IDX_DIM + IDX_HEADS
LANES = 128

F32_MIN_NORMAL = float(2.0 ** -126)
F32_MAX = float((2.0 - 2.0 ** -23) * 2.0 ** 127)
NEG_BIG = -1e30

VMEM_LIMIT = 56 * 1024 * 1024


def _cparams(*sem):
    return pltpu.CompilerParams(dimension_semantics=sem, vmem_limit_bytes=VMEM_LIMIT)


def _dot(a, b):
    return jnp.dot(a, b, preferred_element_type=F32)


def _dot_nt(a, b):
    return lax.dot_general(a, b, (((1,), (1,)), ((), ())), preferred_element_type=F32)


def _dot_exact(a, b):
    return jnp.dot(a, b, preferred_element_type=F32, precision=lax.Precision.HIGHEST)


def _silu(x):
    return x / (1.0 + jnp.exp(-x))


def _softplus(x):
    return jnp.maximum(x, 0.0) + jnp.log(1.0 + jnp.exp(-jnp.abs(x)))


def _layer_norm(y, g, b):
    mu = jnp.mean(y, axis=-1, keepdims=True)
    d = y - mu
    var = jnp.mean(d * d, axis=-1, keepdims=True)
    return d * lax.rsqrt(var + LN_EPS) * g + b


def _swap_halves(x, half):
    w = x.shape[-1]
    lane = lax.broadcasted_iota(jnp.int32, x.shape, x.ndim - 1)
    first = (lane & (2 * half - 1)) < half
    return jnp.where(first, pltpu.roll(x, w - half, x.ndim - 1), pltpu.roll(x, half, x.ndim - 1))


def _inproj_kernel(x_ref, pos_ref, wm_ref, wx_ref, inv64_ref, inv32_ref,
                   q_ref, k_ref, v_ref, iq_ref, z_ref, xbc_ref, misc_ref):
    xb = x_ref[...].astype(BF16)
    pos = pos_ref[...].astype(F32)
    lane = lax.broadcasted_iota(jnp.int32, (1, LANES), 1)

    ang64 = pos * inv64_ref[...]
    cos64 = jnp.cos(ang64)
    sin64 = jnp.where((lane & (HEAD_DIM - 1)) < HEAD_DIM // 2, -1.0, 1.0) * jnp.sin(ang64)
    ang32 = pos * inv32_ref[...]
    cos32 = jnp.cos(ang32)
    sin32 = jnp.where((lane & (IDX_DIM - 1)) < IDX_DIM // 2, -1.0, 1.0) * jnp.sin(ang32)

    def rope(t, cos_t, sin_t, half):
        reps = t.shape[-1] // LANES
        c = jnp.concatenate([cos_t] * reps, axis=1) if reps > 1 else cos_t
        s = jnp.concatenate([sin_t] * reps, axis=1) if reps > 1 else sin_t
        return t * c + _swap_halves(t, half) * s

    o = 0
    q = _dot(xb, wm_ref[:, o:o + ATTN_WIDTH]); o += ATTN_WIDTH
    q_ref[...] = (rope(q, cos64, sin64, HEAD_DIM // 2) * (HEAD_DIM ** -0.5)).astype(BF16)
    k = _dot(xb, wm_ref[:, o:o + ATTN_WIDTH]); o += ATTN_WIDTH
    k_ref[...] = rope(k, cos64, sin64, HEAD_DIM // 2).astype(BF16)
    v_ref[...] = _dot(xb, wm_ref[:, o:o + ATTN_WIDTH]).astype(BF16); o += ATTN_WIDTH
    iq = _dot(xb, wm_ref[:, o:o + IDX_HEADS * IDX_DIM]); o += IDX_HEADS * IDX_DIM
    iq_ref[...] = rope(iq, cos32, sin32, IDX_DIM // 2).astype(BF16)
    z_ref[...] = _dot(xb, wm_ref[:, o:o + SSD_WIDTH]).astype(BF16); o += SSD_WIDTH
    xbc_ref[...] = _dot(xb, wm_ref[:, o:o + XBC_WIDTH]).astype(BF16)
    misc = _dot(xb, wx_ref[...])
    misc_ref[...] = jnp.where(lane < IDX_DIM, rope(misc, cos32, sin32, IDX_DIM // 2), misc)


def _inproj(x2, pos2, w_main, w_misc, inv64, inv32, tm):
    n, d = x2.shape
    wm = w_main.shape[1]
    row = lambda w: pl.BlockSpec((tm, w), lambda i: (i, 0))
    full = lambda a: pl.BlockSpec(a.shape, lambda i: (0, 0))
    outs = [(ATTN_WIDTH, BF16)] * 3 + [(IDX_HEADS * IDX_DIM, BF16), (SSD_WIDTH, BF16),
                                       (XBC_WIDTH, BF16), (LANES, F32)]
    return pl.pallas_call(
        _inproj_kernel,
        grid=(n // tm,),
        in_specs=[row(d), row(1), full(w_main), full(w_misc), full(inv64), full(inv32)],
        out_specs=[row(w) for w, _ in outs],
        out_shape=[jax.ShapeDtypeStruct((n, w), dt) for w, dt in outs],
        compiler_params=_cparams("parallel"),
        name="inproj_rope",
    )(x2, pos2, w_main, w_misc, inv64, inv32)


def _dsa_kernel(q_ref, k_ref, v_ref, iq_ref, mq_ref, mk_ref, o_ref,
                s_scr, iwb_scr, m_scr, l_scr, acc_scr, *, topk, tq):
    tk = tq
    j = pl.program_id(1)
    nch = j + 1
    kf = float(topk)
    ones_k = jnp.ones((tk, LANES), BF16)

    iw = mq_ref[:, MISC_IW:MISC_IW + IDX_HEADS] * (1.0 / 16.0)
    for h in range(IDX_HEADS):
        iwb_scr[h] = jnp.broadcast_to(iw[:, h:h + 1], (tq, LANES))
    iq = iq_ref[...]
    rowpos = j * tq + lax.broadcasted_iota(jnp.int32, (tq, tk), 0)
    colio = lax.broadcasted_iota(jnp.int32, (tq, tk), 1)

    def score_chunk(c, carry):
        off = pl.multiple_of(c * tk, tk)
        ikc = mk_ref[pl.ds(off, tk), MISC_IK:MISC_IK + IDX_DIM].astype(BF16)
        sc = jnp.zeros((tq, tk), F32)
        for h in range(IDX_HEADS):
            d = _dot_nt(iq[:, h * IDX_DIM:(h + 1) * IDX_DIM], ikc)
            w = iwb_scr[h]
            sc = sc + jnp.concatenate([w] * (tk // LANES), axis=1) * jnp.maximum(d, 0.0)
        sc = jnp.where(c * tk + colio <= rowpos, sc, -jnp.inf)
        s_scr[c] = sc
        return carry

    lax.fori_loop(0, nch, score_chunk, 0)

    def count(pred):
        def body(c, acc):
            m = jnp.where(pred(s_scr[c], c), 1.0, 0.0).astype(BF16)
            return acc + _dot(m, ones_k)
        return lax.fori_loop(0, nch, body, jnp.zeros((tq, LANES), F32))

    def wide(t):
        return jnp.concatenate([t] * (tk // LANES), axis=1)

    def count_ge(t):
        tw = wide(t)
        return count(lambda s, c: s >= tw)

    pos_row = count_ge(jnp.zeros((tq, LANES), F32)) >= kf
    sgn = jnp.where(pos_row, 1.0, -1.0)

    def accept(trial_mag):
        ok = jnp.where(count_ge(sgn * trial_mag) >= kf, 1.0, -1.0)
        return ok == sgn

    mag = jnp.where(accept(jnp.full((tq, LANES), F32_MIN_NORMAL, F32)), F32_MIN_NORMAL, 0.0)
    for b in range(7, -1, -1):
        if b == 7:
            trial = (mag * float(2.0 ** 64)) * float(2.0 ** 64)
        else:
            trial = mag * float(2.0 ** (2 ** b))
        trial = jnp.minimum(trial, F32_MAX)
        mag = jnp.where(accept(trial), trial, mag)
    base = mag
    for kbit in range(1, 24):
        trial = jnp.minimum(mag + base * float(2.0 ** -kbit), F32_MAX)
        mag = jnp.where(accept(trial), trial, mag)
    ulp = jnp.where(base > 0.0, base * float(2.0 ** -23), F32_MIN_NORMAL)
    thr = jnp.where(pos_row, mag, -(mag + ulp))

    n_ge = count_ge(thr)

    @pl.when(jnp.max(n_ge) > kf)
    def _():
        tw = wide(thr)
        n_gt = count(lambda s, c: s > tw)
        n_eq = n_ge - n_gt
        target = jnp.minimum(kf - n_gt, n_eq)
        mpos = jnp.zeros((tq, LANES), F32)
        nbits = int(math.ceil(math.log2(tk * s_scr.shape[0]))) + 1
        for b in range(nbits - 1, -1, -1):
            trial = mpos + float(2 ** b)
            trw = wide(trial)
            cnt = count(lambda s, c: jnp.where(s == tw, (c * tk + colio).astype(F32), 1e9) < trw)
            mpos = jnp.where(cnt < target, trial, mpos)
        mw = wide(mpos)

        def drop(c, carry):
            s = s_scr[c]
            kill = jnp.where(s == tw, (c * tk + colio).astype(F32), -1.0) > mw
            s_scr[c] = jnp.where(kill, -jnp.inf, s)
            return carry

        lax.fori_loop(0, nch, drop, 0)

    thr_w = wide(jnp.maximum(thr, -F32_MAX))
    m_scr[...] = jnp.full(m_scr.shape, NEG_BIG, F32)
    l_scr[...] = jnp.zeros(l_scr.shape, F32)
    acc_scr[...] = jnp.zeros(acc_scr.shape, F32)

    def attn_chunk(c, carry):
        off = pl.multiple_of(c * tk, tk)
        sel = s_scr[c] >= thr_w
        for h in range(ATTN_HEADS):
            hs = slice(h * HEAD_DIM, (h + 1) * HEAD_DIM)
            lg = _dot_nt(q_ref[:, hs], k_ref[pl.ds(off, tk), hs])
            m_old = m_scr[h]
            m_new = jnp.maximum(m_old, jnp.max(jnp.where(sel, lg, NEG_BIG), axis=1, keepdims=True))
            p = jnp.where(sel, jnp.exp(lg - wide(m_new)), 0.0)
            alpha = jnp.exp(m_old - m_new)
            l_scr[h] = alpha * l_scr[h] + jnp.sum(p, axis=1, keepdims=True)
            acc_scr[h] = alpha[:, :HEAD_DIM] * acc_scr[h] + _dot(p.astype(BF16), v_ref[pl.ds(off, tk), hs])
            m_scr[h] = m_new
        return carry

    lax.fori_loop(0, nch, attn_chunk, 0)
    for h in range(ATTN_HEADS):
        o_ref[:, h * HEAD_DIM:(h + 1) * HEAD_DIM] = (acc_scr[h] / l_scr[h][:, :HEAD_DIM]).astype(BF16)


def _dsa(q, k, v, iq, misc, bsz, seq, tq):
    n = bsz * seq
    nq = seq // tq
    topk = min(MAX_TOPK_KEYS, seq // 4)
    qblk = lambda w: pl.BlockSpec((tq, w), lambda b, j: (b * nq + j, 0))
    kblk = lambda w: pl.BlockSpec((seq, w), lambda b, j: (b, 0))
    return pl.pallas_call(
        functools.partial(_dsa_kernel, topk=topk, tq=tq),
        grid=(bsz, nq),
        in_specs=[qblk(ATTN_WIDTH), kblk(ATTN_WIDTH), kblk(ATTN_WIDTH),
                  qblk(IDX_HEADS * IDX_DIM), qblk(LANES), kblk(LANES)],
        out_specs=qblk(ATTN_WIDTH),
        out_shape=jax.ShapeDtypeStruct((n, ATTN_WIDTH), BF16),
        scratch_shapes=[
            pltpu.VMEM((nq, tq, tq), F32),
            pltpu.VMEM((IDX_HEADS, tq, LANES), F32),
            pltpu.VMEM((ATTN_HEADS, tq, LANES), F32),
            pltpu.VMEM((ATTN_HEADS, tq, LANES), F32),
            pltpu.VMEM((ATTN_HEADS, tq, HEAD_DIM), F32),
        ],
        compiler_params=_cparams("parallel", "arbitrary"),
        name="dsa_attention",
    )(q, k, v, iq, misc, misc)


def _ssd_kernel(z_ref, xbc_ref, misc_ref, cw_ref, cb_ref, dtb_ref, alog_ref, dsk_ref, nw_ref, e_ref,
                o_ref, tail_scr, st_scr):
    seq = z_ref.shape[0]
    t = CHUNK
    tail_scr[...] = jnp.zeros(tail_scr.shape, F32)
    st_scr[...] = jnp.zeros(st_scr.shape, F32)
    a_lane = -jnp.exp(alog_ref[...])
    r = lax.broadcasted_iota(jnp.int32, (t, t), 0)
    cidx = lax.broadcasted_iota(jnp.int32, (t, t), 1)
    lower = r >= cidx
    ltri = jnp.where(lower, 1.0, 0.0)
    expand = e_ref[...]
    gw = SSD_WIDTH // SSD_GROUPS
    hpg = SSD_HEADS // SSD_GROUPS

    def chunk(c, carry):
        off = pl.multiple_of(c * t, t)
        xin = xbc_ref[pl.ds(off, t), :].astype(F32)
        xp = jnp.concatenate([tail_scr[...], xin], axis=0)
        tail_scr[...] = xin[t - 8:, :]
        acc = jnp.broadcast_to(cb_ref[...], (t, XBC_WIDTH))
        for w in range(CONV_WIDTH):
            s0 = 8 - (CONV_WIDTH - 1) + w
            acc = acc + xp[s0:s0 + t, :] * cw_ref[w:w + 1, :]
        xc = _silu(acc)
        xs = xc[:, :SSD_WIDTH]
        bm = xc[:, SSD_WIDTH:SSD_WIDTH + SSD_GROUPS * D_STATE]
        cm = xc[:, SSD_WIDTH + SSD_GROUPS * D_STATE:]

        dt = _softplus(misc_ref[pl.ds(off, t), :] + dtb_ref[...])
        acum = _dot_exact(ltri, dt * a_lane)
        acum_t = acum.T
        dt_e = _dot_exact(dt, expand)
        ac_e = _dot_exact(acum, expand)
        alast_e = ac_e[t - 1:t, :]
        xdt = xs * dt_e
        xw = (xdt * jnp.exp(alast_e - ac_e)).astype(BF16)
        eac = jnp.exp(ac_e)
        xdt_b = xdt.astype(BF16)

        ys = []
        for g in range(SSD_GROUPS):
            gs = slice(g * gw, (g + 1) * gw)
            bg = bm[:, g * D_STATE:(g + 1) * D_STATE]
            cg = cm[:, g * D_STATE:(g + 1) * D_STATE].astype(BF16)
            cb = _dot_nt(cg, bg.astype(BF16))
            st_old = st_scr[:, gs]
            y_off = _dot(cg, st_old.astype(BF16)) * eac[:, gs]
            st_scr[:, gs] = jnp.exp(alast_e[:, gs]) * st_old + _dot(bg.T.astype(BF16), xw[:, gs])
            for hh in range(hpg):
                h = g * hpg + hh
                col = acum[:, MISC_DT + h:MISC_DT + h + 1]
                row = acum_t[MISC_DT + h:MISC_DT + h + 1, :]
                decay = jnp.exp(jnp.where(lower, col - row, -jnp.inf))
                wmat = (cb * decay).astype(BF16)
                hs = slice(h * SSD_HEAD_DIM, (h + 1) * SSD_HEAD_DIM)
                ys.append(_dot(wmat, xdt_b[:, hs]) + y_off[:, hh * SSD_HEAD_DIM:(hh + 1) * SSD_HEAD_DIM])
        y = jnp.concatenate(ys, axis=1) + xs * dsk_ref[...]
        y = y * _silu(z_ref[pl.ds(off, t), :].astype(F32))
        outs = []
        for g in range(SSD_GROUPS):
            yg = y[:, g * gw:(g + 1) * gw]
            outs.append(yg * lax.rsqrt(jnp.mean(yg * yg, axis=-1, keepdims=True) + LN_EPS))
        o_ref[pl.ds(off, t), :] = (jnp.concatenate(outs, axis=1) * nw_ref[...]).astype(BF16)
        return carry

    lax.fori_loop(0, seq // t, chunk, 0)


def _ssd(z, xbc, misc, conv_w, conv_b, dtb_lane, alog_lane, dsk, norm_w, expand, bsz, seq):
    n = bsz * seq
    blk = lambda w: pl.BlockSpec((seq, w), lambda b: (b, 0))
    full = lambda a: pl.BlockSpec(a.shape, lambda b: (0, 0))
    return pl.pallas_call(
        _ssd_kernel,
        grid=(bsz,),
        in_specs=[blk(SSD_WIDTH), blk(XBC_WIDTH), blk(LANES), full(conv_w), full(conv_b),
                  full(dtb_lane), full(alog_lane), full(dsk), full(norm_w), full(expand)],
        out_specs=blk(SSD_WIDTH),
        out_shape=jax.ShapeDtypeStruct((n, SSD_WIDTH), BF16),
        scratch_shapes=[pltpu.VMEM((8, XBC_WIDTH), F32),
                        pltpu.VMEM((D_STATE, SSD_WIDTH), F32)],
        compiler_params=_cparams("parallel"),
        name="ssd_mixer",
    )(z, xbc, misc, conv_w, conv_b, dtb_lane, alog_lane, dsk, norm_w, expand)


def _outproj_kernel(attn_ref, ssd_ref, x_ref, wa_ref, ws_ref, g_ref, b_ref, h_ref, hb_ref, *, alpha):
    mix = _dot(attn_ref[...], wa_ref[...]) + _dot(ssd_ref[...], ws_ref[...])
    h = _layer_norm(alpha * x_ref[...] + mix, g_ref[...], b_ref[...])
    h_ref[...] = h
    hb_ref[...] = h.astype(BF16)


def _outproj(attn, ssd, x2, wa, ws, g, b, alpha, tm):
    n, d = x2.shape
    row = lambda w: pl.BlockSpec((tm, w), lambda i: (i, 0))
    full = lambda a: pl.BlockSpec(a.shape, lambda i: (0, 0))
    return pl.pallas_call(
        functools.partial(_outproj_kernel, alpha=alpha),
        grid=(n // tm,),
        in_specs=[row(ATTN_WIDTH), row(SSD_WIDTH), row(d), full(wa), full(ws), full(g), full(b)],
        out_specs=[row(d), row(d)],
        out_shape=[jax.ShapeDtypeStruct((n, d), F32), jax.ShapeDtypeStruct((n, d), BF16)],
        compiler_params=_cparams("parallel"),
        name="outproj_ln1",
    )(attn, ssd, x2, wa, ws, g, b)


def _peer_score_kernel(hb_ref, wq_ref, k1_ref, k2_ref, s1_ref, s2_ref):
    qb = _dot(hb_ref[...], wq_ref[...]).astype(BF16)
    k1 = k1_ref[...]
    k2 = k2_ref[...]
    for h in range(PEER_HEADS):
        o = h * 2 * PEER_HALF
        s1_ref[h] = _dot_nt(k1, qb[:, o:o + PEER_HALF])
        s2_ref[h] = _dot_nt(k2, qb[:, o + PEER_HALF:o + 2 * PEER_HALF])


def _peer_scores(hb, wq, k1, k2, tm):
    n, d = hb.shape
    full = lambda a: pl.BlockSpec(a.shape, lambda i: (0, 0))
    sblk = pl.BlockSpec((PEER_HEADS, N_KEYS, tm), lambda i: (0, 0, i))
    sshape = jax.ShapeDtypeStruct((PEER_HEADS, N_KEYS, n), F32)
    return pl.pallas_call(
        _peer_score_kernel,
        grid=(n // tm,),
        in_specs=[pl.BlockSpec((tm, d), lambda i: (i, 0)), full(wq), full(k1), full(k2)],
        out_specs=[sblk, sblk],
        out_shape=[sshape, sshape],
        compiler_params=_cparams("parallel"),
        name="peer_scores",
    )(hb, wq, k1, k2)


_CAND_COUNTS = [PEER_TOPK // (j1 + 1) for j1 in range(PEER_SUB_TOPK)]
_CAND_ROWS = sum(_CAND_COUNTS)
_CAND_PAD = -(-_CAND_ROWS // 8) * 8


def _peer_select_kernel(s1_ref, s2_ref, pos_ref, ea_ref, n1_ref, eb_ref, r2_ref, v_scr, cand_scr):
    tl = s1_ref.shape[2]
    kio = lax.broadcasted_iota(jnp.int32, (N_KEYS, tl), 0).astype(F32)
    cpos = jnp.broadcast_to(pos_ref[...], (_CAND_PAD, tl))

    def top_ranks(s, slot):
        rank = jnp.full((N_KEYS, tl), float(PEER_SUB_TOPK), F32)
        for jj in range(PEER_SUB_TOPK):
            m = jnp.max(s, axis=0, keepdims=True)
            idx = jnp.min(jnp.where(s == m, kio, float(N_KEYS)), axis=0, keepdims=True)
            hit = kio == idx
            s = jnp.where(hit, -jnp.inf, s)
            rank = jnp.where(hit, float(jj), rank)
            v_scr[slot, jj:jj + 1, :] = m
        return rank

    def head(h, carry):
        s1 = s1_ref[h]
        s2 = s2_ref[h]
        rank1 = top_ranks(s1, 0)
        rank2 = top_ranks(s2, 1)
        v1 = v_scr[0]
        v2 = v_scr[1]
        o = 0
        for j1, cnt in enumerate(_CAND_COUNTS):
            cand_scr[o:o + cnt, :] = v1[j1:j1 + 1, :] + v2[0:cnt, :]
            o += cnt
        if _CAND_PAD > _CAND_ROWS:
            cand_scr[_CAND_ROWS:, :] = jnp.full((_CAND_PAD - _CAND_ROWS, tl), -jnp.inf, F32)
        cand = cand_scr[...]
        picked = jnp.zeros((_CAND_PAD, tl), F32)
        zsum = jnp.zeros((1, tl), F32)
        best0 = None
        for kk in range(PEER_TOPK):
            m = jnp.max(cand, axis=0, keepdims=True)
            idx = jnp.min(jnp.where(cand == m, cpos, 1e9), axis=0, keepdims=True)
            hit = cpos == idx
            cand = jnp.where(hit, -jnp.inf, cand)
            picked = jnp.where(hit, 1.0, picked)
            if kk == 0:
                best0 = m
            zsum = zsum + jnp.exp(m - best0)
        n1 = jnp.zeros((N_KEYS, tl), F32)
        cand_scr[...] = picked
        o = 0
        for j1, cnt in enumerate(_CAND_COUNTS):
            nsel = jnp.sum(cand_scr[o:o + cnt, :], axis=0, keepdims=True)
            n1 = jnp.where(rank1 == float(j1), nsel, n1)
            o += cnt
        ea_ref[h] = jnp.exp(s1 - v1[0:1, :]) / zsum
        n1_ref[h] = n1
        eb_ref[h] = jnp.exp(s2 - v2[0:1, :]).astype(BF16)
        r2_ref[h] = rank2.astype(BF16)
        return carry

    lax.fori_loop(0, PEER_HEADS, head, 0)


def _peer_select(s1t, s2t, cand_pos, tl):
    n = s1t.shape[2]
    blk = pl.BlockSpec((PEER_HEADS, N_KEYS, tl), lambda i: (0, 0, i))
    shp = lambda dt: jax.ShapeDtypeStruct((PEER_HEADS, N_KEYS, n), dt)
    return pl.pallas_call(
        _peer_select_kernel,
        grid=(n // tl,),
        in_specs=[blk, blk, pl.BlockSpec(cand_pos.shape, lambda i: (0, 0))],
        out_specs=[blk, blk, blk, blk],
        out_shape=[shp(F32), shp(F32), shp(BF16), shp(BF16)],
        scratch_shapes=[pltpu.VMEM((2, PEER_SUB_TOPK, tl), F32), pltpu.VMEM((_CAND_PAD, tl), F32)],
        compiler_params=_cparams("parallel"),
        name="peer_select",
    )(s1t, s2t, cand_pos)


def _peer_expert_kernel(ht_ref, u_ref, vt_ref, ea_ref, n1_ref, eb_ref, r2_ref, o_ref, act_scr, acc_scr):
    e = pl.program_id(1)
    es = u_ref.shape[0]
    tl = ht_ref.shape[1]

    @pl.when(e == 0)
    def _():
        acc_scr[...] = jnp.zeros(acc_scr.shape, F32)

    pt = _dot(u_ref[...], ht_ref[...])
    for cc in range(es // N_KEYS):
        g = jnp.zeros((N_KEYS, tl), F32)
        for h in range(PEER_HEADS):
            take = r2_ref[h].astype(F32) < n1_ref[h, cc:cc + 1, :]
            g = g + jnp.where(take, eb_ref[h].astype(F32), 0.0) * ea_ref[h, cc:cc + 1, :]
        p = pt[cc * N_KEYS:(cc + 1) * N_KEYS, :]
        gelu = 0.5 * p * (1.0 + lax.erf(p * (2.0 ** -0.5)))
        act_scr[cc * N_KEYS:(cc + 1) * N_KEYS, :] = (gelu * g).astype(BF16)
    acc_scr[...] += _dot(vt_ref[...], act_scr[...])

    @pl.when(e == pl.num_programs(1) - 1)
    def _():
        o_ref[...] = acc_scr[...]


def _peer_experts(ht, u, vt, ea, n1, eb, r2, tl, es):
    d, n = ht.shape
    ne = u.shape[0]
    rows = es // N_KEYS
    a_blk = pl.BlockSpec((PEER_HEADS, rows, tl), lambda i, e: (0, e, i))
    b_blk = pl.BlockSpec((PEER_HEADS, N_KEYS, tl), lambda i, e: (0, 0, i))
    return pl.pallas_call(
        _peer_expert_kernel,
        grid=(n // tl, ne // es),
        in_specs=[pl.BlockSpec((d, tl), lambda i, e: (0, i)),
                  pl.BlockSpec((es, d), lambda i, e: (e, 0)),
                  pl.BlockSpec((d, es), lambda i, e: (0, e)),
                  a_blk, a_blk, b_blk, b_blk],
        out_specs=pl.BlockSpec((d, tl), lambda i, e: (0, i)),
        out_shape=jax.ShapeDtypeStruct((d, n), F32),
        scratch_shapes=[pltpu.VMEM((es, tl), BF16), pltpu.VMEM((d, tl), F32)],
        compiler_params=_cparams("parallel", "arbitrary"),
        name="peer_experts",
    )(ht, u, vt, ea, n1, eb, r2)


def _ln2_kernel(h_ref, f_ref, g_ref, b_ref, o_ref, *, alpha):
    o_ref[...] = _layer_norm(alpha * h_ref[...] + f_ref[...], g_ref[...], b_ref[...])


def _ln2(h, ffn, g, b, alpha, tm):
    n, d = h.shape
    row = pl.BlockSpec((tm, d), lambda i: (i, 0))
    full = lambda a: pl.BlockSpec(a.shape, lambda i: (0, 0))
    return pl.pallas_call(
        functools.partial(_ln2_kernel, alpha=alpha),
        grid=(n // tm,),
        in_specs=[row, row, full(g), full(b)],
        out_specs=row,
        out_shape=jax.ShapeDtypeStruct((n, d), F32),
        compiler_params=_cparams("parallel"),
        name="residual_ln2",
    )(h, ffn, g, b)


def _pick(n, pref):
    t = min(pref, n)
    while n % t:
        t //= 2
    return t


def _layer(x, positions, w_in, conv_w, conv_b, dt_bias, a_log, d_skip, ssm_norm_w, w_out,
           ln1_g, ln1_b, peer_wq, peer_k1, peer_k2, peer_u, peer_v, ln2_g, ln2_b, alpha):
    bsz, seq, d = x.shape
    n = bsz * seq
    x2 = x.reshape(n, d)
    pos2 = positions.reshape(n, 1)

    sizes = (ATTN_WIDTH, ATTN_WIDTH, ATTN_WIDTH, IDX_HEADS * IDX_DIM, IDX_DIM, IDX_HEADS,
             SSD_WIDTH, XBC_WIDTH, SSD_HEADS)
    offs = [0]
    for s in sizes:
        offs.append(offs[-1] + s)
    col = lambda i: w_in[:, offs[i]:offs[i + 1]]
    w_main = jnp.concatenate([col(0), col(1), col(2), col(3), col(6), col(7)], axis=1).astype(BF16)
    w_misc = jnp.concatenate(
        [col(4), col(5), col(8), jnp.zeros((d, LANES - IDX_DIM - IDX_HEADS - SSD_HEADS), w_in.dtype)],
        axis=1).astype(BF16)
    lane = jnp.arange(LANES)
    inv64 = (ROPE_THETA ** (-jnp.arange(0, HEAD_DIM, 2, dtype=F32) / HEAD_DIM))[(lane % HEAD_DIM) % (HEAD_DIM // 2)]
    inv32 = (ROPE_THETA ** (-jnp.arange(0, IDX_DIM, 2, dtype=F32) / IDX_DIM))[(lane % IDX_DIM) % (IDX_DIM // 2)]

    q, k, v, iq, z, xbc, misc = _inproj(x2, pos2, w_main, w_misc, inv64[None, :], inv32[None, :], _pick(n, 512))

    attn = _dsa(q, k, v, iq, misc, bsz, seq, _pick(seq, 256))

    pad_dt = lambda a: jnp.zeros((1, LANES), F32).at[0, MISC_DT:MISC_DT + SSD_HEADS].set(a.astype(F32))
    expand = jnp.zeros((LANES, SSD_WIDTH), F32).at[
        MISC_DT + jnp.arange(SSD_WIDTH) // SSD_HEAD_DIM, jnp.arange(SSD_WIDTH)].set(1.0)
    ssd = _ssd(z, xbc, misc, conv_w.astype(F32), conv_b.astype(F32)[None, :], pad_dt(dt_bias), pad_dt(a_log),
               jnp.repeat(d_skip.astype(F32), SSD_HEAD_DIM)[None, :], ssm_norm_w.astype(F32)[None, :],
               expand, bsz, seq)

    wob = w_out.astype(BF16)
    h, hb = _outproj(attn, ssd, x2, wob[:ATTN_WIDTH], wob[ATTN_WIDTH:], ln1_g[None, :], ln1_b[None, :],
                     alpha, _pick(n, 512))

    s1t, s2t = _peer_scores(hb, peer_wq.astype(BF16), peer_k1.astype(BF16), peer_k2.astype(BF16), _pick(n, 512))

    cand_pos = []
    for j1, cnt in enumerate(_CAND_COUNTS):
        cand_pos += [float(j1 * PEER_SUB_TOPK + j2) for j2 in range(cnt)]
    cand_pos += [1e9] * (_CAND_PAD - _CAND_ROWS)
    ea, n1, eb, r2 = _peer_select(s1t, s2t, jnp.asarray(cand_pos, F32)[:, None], _pick(n, 256))

    ffn_t = _peer_experts(hb.T, peer_u.astype(BF16), peer_v.T.astype(BF16), ea, n1, eb, r2,
                          _pick(n, 512), 8 * N_KEYS)

    out = _ln2(h, ffn_t.T, ln2_g[None, :], ln2_b[None, :], alpha, _pick(n, 512))
    return out.reshape(bsz, seq, d)


def kernel(x, positions, w_in, conv_w, conv_b, dt_bias, a_log, d_skip, ssm_norm_w, w_out, ln1_g, ln1_b,
           peer_wq, peer_k1, peer_k2, peer_u, peer_v, ln2_g, ln2_b):
    depth = w_in.shape[0]
    alpha = float((2 * depth) ** 0.25)
    for i in range(depth):
        x = _layer(x, positions, w_in[i], conv_w[i], conv_b[i], dt_bias[i], a_log[i], d_skip[i],
                   ssm_norm_w[i], w_out[i], ln1_g[i], ln1_b[i], peer_wq[i], peer_k1[i], peer_k2[i],
                   peer_u[i], peer_v[i], ln2_g[i], ln2_b[i], alpha)
    return x
```

```python
import functools
import math

import jax
import jax.numpy as jnp
from jax import lax
from jax.experimental import pallas as pl
from jax.experimental.pallas import tpu as pltpu

F32 = jnp.float32
BF16 = jnp.bfloat16

ATTN_HEADS = 8
HEAD_DIM = 64
ATTN_WIDTH = ATTN_HEADS * HEAD_DIM
IDX_HEADS = 8
IDX_DIM = 32
MAX_TOPK_KEYS = 256
SSD_HEADS = 8
SSD_HEAD_DIM = 64
SSD_WIDTH = SSD_HEADS * SSD_HEAD_DIM
SSD_GROUPS = 2
D_STATE = 128
CONV_WIDTH = 4
CHUNK = 128
XBC_WIDTH = SSD_WIDTH + 2 * SSD_GROUPS * D_STATE
PEER_HEADS = 8
N_KEYS = 128
PEER_HALF = 128
PEER_SUB_TOPK = 16
PEER_TOPK = 16
ROPE_THETA = 10000.0
LN_EPS = 1e-5

MISC_IK = 0
MISC_IW = IDX_DIM
MISC_DT = IDX_DIM + IDX_HEADS
LANES = 128

F32_MIN_NORMAL = float(2.0 ** -126)
F32_MAX = float((2.0 - 2.0 ** -23) * 2.0 ** 127)
NEG_BIG = -1e30

VMEM_LIMIT = 56 * 1024 * 1024


def _cparams(*sem):
    return pltpu.CompilerParams(dimension_semantics=sem, vmem_limit_bytes=VMEM_LIMIT)


def _dot(a, b):
    return jnp.dot(a, b, preferred_element_type=F32)


def _dot_nt(a, b):
    return lax.dot_general(a, b, (((1,), (1,)), ((), ())), preferred_element_type=F32)


def _dot_exact(a, b):
    return jnp.dot(a, b, preferred_element_type=F32, precision=lax.Precision.HIGHEST)


def _silu(x):
    return x / (1.0 + jnp.exp(-x))


def _softplus(x):
    return jnp.maximum(x, 0.0) + jnp.log(1.0 + jnp.exp(-jnp.abs(x)))


def _layer_norm(y, g, b):
    mu = jnp.mean(y, axis=-1, keepdims=True)
    d = y - mu
    var = jnp.mean(d * d, axis=-1, keepdims=True)
    return d * lax.rsqrt(var + LN_EPS) * g + b


def _swap_halves(x, half):
    w = x.shape[-1]
    lane = lax.broadcasted_iota(jnp.int32, x.shape, x.ndim - 1)
    first = (lane & (2 * half - 1)) < half
    return jnp.where(first, pltpu.roll(x, w - half, x.ndim - 1), pltpu.roll(x, half, x.ndim - 1))


def _inproj_kernel(x_ref, pos_ref, wm_ref, wx_ref, inv64_ref, inv32_ref,
                   q_ref, k_ref, va_ref, iq_ref, z_ref, xbc_ref, misc_ref):
    xb = x_ref[...].astype(BF16)
    pos = pos_ref[...].astype(F32)
    lane = lax.broadcasted_iota(jnp.int32, (1, LANES), 1)

    ang64 = pos * inv64_ref[...]
    cos64 = jnp.cos(ang64)
    sin64 = jnp.where((lane & (HEAD_DIM - 1)) < HEAD_DIM // 2, -1.0, 1.0) * jnp.sin(ang64)
    ang32 = pos * inv32_ref[...]
    cos32 = jnp.cos(ang32)
    sin32 = jnp.where((lane & (IDX_DIM - 1)) < IDX_DIM // 2, -1.0, 1.0) * jnp.sin(ang32)

    def rope(t, cos_t, sin_t, half):
        reps = t.shape[-1] // LANES
        c = jnp.concatenate([cos_t] * reps, axis=1) if reps > 1 else cos_t
        s = jnp.concatenate([sin_t] * reps, axis=1) if reps > 1 else sin_t
        return t * c + _swap_halves(t, half) * s

    o = 0
    q = _dot(xb, wm_ref[:, o:o + ATTN_WIDTH]); o += ATTN_WIDTH
    q_ref[...] = (rope(q, cos64, sin64, HEAD_DIM // 2) * (HEAD_DIM ** -0.5)).astype(BF16)
    k = _dot(xb, wm_ref[:, o:o + ATTN_WIDTH]); o += ATTN_WIDTH
    k_ref[...] = rope(k, cos64, sin64, HEAD_DIM // 2).astype(BF16)
    va = _dot(xb, wm_ref[:, o:o + ATTN_HEADS * LANES]); o += ATTN_HEADS * LANES
    pad = lax.broadcasted_iota(jnp.int32, (1, ATTN_HEADS * LANES), 1) & (LANES - 1)
    va_ref[...] = jnp.where(pad >= HEAD_DIM, 1.0, va).astype(BF16)
    iq = _dot(xb, wm_ref[:, o:o + IDX_HEADS * IDX_DIM]); o += IDX_HEADS * IDX_DIM
    iq_ref[...] = rope(iq, cos32, sin32, IDX_DIM // 2).astype(BF16)
    z_ref[...] = _dot(xb, wm_ref[:, o:o + SSD_WIDTH]).astype(BF16); o += SSD_WIDTH
    xbc_ref[...] = _dot(xb, wm_ref[:, o:o + XBC_WIDTH]).astype(BF16)
    misc = _dot(xb, wx_ref[...])
    misc_ref[...] = jnp.where(lane < IDX_DIM, rope(misc, cos32, sin32, IDX_DIM // 2), misc)


def _inproj(x2, pos2, w_main, w_misc, inv64, inv32, tm):
    n, d = x2.shape
    wm = w_main.shape[1]
    row = lambda w: pl.BlockSpec((tm, w), lambda i: (i, 0))
    full = lambda a: pl.BlockSpec(a.shape, lambda i: (0, 0))
    outs = [(ATTN_WIDTH, BF16), (ATTN_WIDTH, BF16), (ATTN_HEADS * LANES, BF16),
            (IDX_HEADS * IDX_DIM, BF16), (SSD_WIDTH, BF16), (XBC_WIDTH, BF16), (LANES, F32)]
    return pl.pallas_call(
        _inproj_kernel,
        grid=(n // tm,),
        in_specs=[row(d), row(1), full(w_main), full(w_misc), full(inv64), full(inv32)],
        out_specs=[row(w) for w, _ in outs],
        out_shape=[jax.ShapeDtypeStruct((n, w), dt) for w, dt in outs],
        compiler_params=_cparams("parallel"),
        name="inproj_rope",
    )(x2, pos2, w_main, w_misc, inv64, inv32)


def _dsa_kernel(q_ref, k_ref, va_ref, iq_ref, mq_ref, mk_ref, o_ref, s_scr, m_scr, acc_scr, *, topk, tq):
    tk = tq
    j = pl.program_id(1)
    nch = j + 1
    kf = float(topk)

    iw_t = mq_ref[...].T[MISC_IW:MISC_IW + IDX_HEADS, :] * (1.0 / 16.0)
    iq = iq_ref[...]
    qpos = j * tq + lax.broadcasted_iota(jnp.int32, (tk, tq), 1)
    kio = lax.broadcasted_iota(jnp.int32, (tk, tq), 0)

    def score_chunk(c, carry):
        off = pl.multiple_of(c * tk, tk)
        ikc = mk_ref[pl.ds(off, tk), MISC_IK:MISC_IK + IDX_DIM].astype(BF16)
        sc = jnp.zeros((tk, tq), F32)
        for h in range(IDX_HEADS):
            d = _dot_nt(ikc, iq[:, h * IDX_DIM:(h + 1) * IDX_DIM])
            sc = sc + iw_t[h:h + 1, :] * jnp.maximum(d, 0.0)
        s_scr[c] = jnp.where(c * tk + kio <= qpos, sc, -jnp.inf)
        return carry

    lax.fori_loop(0, nch, score_chunk, 0)

    def count(pred):
        def body(c, acc):
            s3 = s_scr[c].reshape(tk // 8, 8, tq)
            return acc + jnp.sum(jnp.where(pred(s3, c), 1.0, 0.0), axis=0)
        acc = lax.fori_loop(0, nch, body, jnp.zeros((8, tq), F32))
        return jnp.sum(acc, axis=0, keepdims=True)

    def rows8(t):
        return jnp.broadcast_to(t, (8, tq))[None]

    def count_ge(t):
        t8 = rows8(t)
        return count(lambda s3, c: s3 >= t8)

    pos_row = count_ge(jnp.zeros((1, tq), F32)) >= kf
    sgn = jnp.where(pos_row, 1.0, -1.0)

    def accept(trial_mag):
        ok = jnp.where(count_ge(sgn * trial_mag) >= kf, 1.0, -1.0)
        return ok == sgn

    mag = jnp.where(accept(jnp.full((1, tq), F32_MIN_NORMAL, F32)), F32_MIN_NORMAL, 0.0)
    for b in range(7, -1, -1):
        if b == 7:
            trial = (mag * float(2.0 ** 64)) * float(2.0 ** 64)
        else:
            trial = mag * float(2.0 ** (2 ** b))
        trial = jnp.minimum(trial, F32_MAX)
        mag = jnp.where(accept(trial), trial, mag)
    base = mag
    for kbit in range(1, 24):
        trial = jnp.minimum(mag + base * float(2.0 ** -kbit), F32_MAX)
        mag = jnp.where(accept(trial), trial, mag)
    ulp = jnp.where(base > 0.0, base * float(2.0 ** -23), F32_MIN_NORMAL)
    thr = jnp.where(pos_row, mag, -(mag + ulp))

    n_ge = count_ge(thr)

    @pl.when(jnp.max(n_ge) > kf)
    def _():
        t8 = rows8(thr)
        kio3 = kio.reshape(tk // 8, 8, tq)
        n_gt = count(lambda s3, c: s3 > t8)
        n_eq = n_ge - n_gt
        target = jnp.minimum(kf - n_gt, n_eq)
        mpos = jnp.zeros((1, tq), F32)
        nbits = int(math.ceil(math.log2(tk * s_scr.shape[0]))) + 1
        for b in range(nbits - 1, -1, -1):
            trial = mpos + float(2 ** b)
            tr8 = rows8(trial)
            cnt = count(lambda s3, c: jnp.where(s3 == t8, (c * tk + kio3).astype(F32), 1e9) < tr8)
            mpos = jnp.where(cnt < target, trial, mpos)

        def drop(c, carry):
            s = s_scr[c]
            kill = jnp.where(s == thr, (c * tk + kio).astype(F32), -1.0) > mpos
            s_scr[c] = jnp.where(kill, -jnp.inf, s)
            return carry

        lax.fori_loop(0, nch, drop, 0)

    def wide(t):
        return jnp.concatenate([t] * (tk // LANES), axis=1)

    thr_q = jnp.broadcast_to(jnp.maximum(thr, -F32_MAX), (LANES, tq)).T
    thr_w = wide(thr_q)
    m_scr[...] = jnp.full(m_scr.shape, NEG_BIG, F32)
    acc_scr[...] = jnp.zeros(acc_scr.shape, F32)

    def attn_chunk(c, carry):
        off = pl.multiple_of(c * tk, tk)
        sel = s_scr[c].T >= thr_w
        for h in range(ATTN_HEADS):
            hs = slice(h * HEAD_DIM, (h + 1) * HEAD_DIM)
            lg = _dot_nt(q_ref[:, hs], k_ref[pl.ds(off, tk), hs])
            lg = jnp.where(sel, lg, -jnp.inf)
            m_old = m_scr[h]
            m_new = jnp.maximum(m_old, jnp.max(lg, axis=1, keepdims=True))
            p = jnp.exp(lg - wide(m_new)).astype(BF16)
            pv = _dot(p, va_ref[pl.ds(off, tk), h * LANES:(h + 1) * LANES])
            acc_scr[h] = jnp.exp(m_old - m_new) * acc_scr[h] + pv
            m_scr[h] = m_new
        return carry

    lax.fori_loop(0, nch, attn_chunk, 0)
    for h in range(ATTN_HEADS):
        a = acc_scr[h]
        o_ref[:, h * HEAD_DIM:(h + 1) * HEAD_DIM] = (a[:, :HEAD_DIM] / a[:, HEAD_DIM:]).astype(BF16)


def _dsa(q, k, va, iq, misc, bsz, seq, tq):
    n = bsz * seq
    nq = seq // tq
    topk = min(MAX_TOPK_KEYS, seq // 4)
    qblk = lambda w: pl.BlockSpec((tq, w), lambda b, j: (b * nq + j, 0))
    kblk = lambda w: pl.BlockSpec((seq, w), lambda b, j: (b, 0))
    return pl.pallas_call(
        functools.partial(_dsa_kernel, topk=topk, tq=tq),
        grid=(bsz, nq),
        in_specs=[qblk(ATTN_WIDTH), kblk(ATTN_WIDTH), kblk(ATTN_HEADS * LANES),
                  qblk(IDX_HEADS * IDX_DIM), qblk(LANES), kblk(LANES)],
        out_specs=qblk(ATTN_WIDTH),
        out_shape=jax.ShapeDtypeStruct((n, ATTN_WIDTH), BF16),
        scratch_shapes=[
            pltpu.VMEM((nq, tq, tq), F32),
            pltpu.VMEM((ATTN_HEADS, tq, LANES), F32),
            pltpu.VMEM((ATTN_HEADS, tq, LANES), F32),
        ],
        compiler_params=_cparams("parallel", "arbitrary"),
        name="dsa_attention",
    )(q, k, va, iq, misc, misc)


def _ssd_kernel(z_ref, xbc_ref, misc_ref, cw_ref, cb_ref, dtb_ref, alog_ref, dsk_ref, nw_ref, e_ref,
                o_ref, tail_scr, st_scr):
    seq = z_ref.shape[0]
    t = CHUNK
    tail_scr[...] = jnp.zeros(tail_scr.shape, F32)
    st_scr[...] = jnp.zeros(st_scr.shape, F32)
    a_lane = -jnp.exp(alog_ref[...])
    r = lax.broadcasted_iota(jnp.int32, (t, t), 0)
    cidx = lax.broadcasted_iota(jnp.int32, (t, t), 1)
    lower = r >= cidx
    ltri = jnp.where(lower, 1.0, 0.0)
    expand = e_ref[...]
    gw = SSD_WIDTH // SSD_GROUPS
    hpg = SSD_HEADS // SSD_GROUPS

    def chunk(c, carry):
        off = pl.multiple_of(c * t, t)
        xin = xbc_ref[pl.ds(off, t), :].astype(F32)
        xp = jnp.concatenate([tail_scr[...], xin], axis=0)
        tail_scr[...] = xin[t - 8:, :]
        acc = jnp.broadcast_to(cb_ref[...], (t, XBC_WIDTH))
        for w in range(CONV_WIDTH):
            s0 = 8 - (CONV_WIDTH - 1) + w
            acc = acc + xp[s0:s0 + t, :] * cw_ref[w:w + 1, :]
        xc = _silu(acc)
        xs = xc[:, :SSD_WIDTH]
        bm = xc[:, SSD_WIDTH:SSD_WIDTH + SSD_GROUPS * D_STATE]
        cm = xc[:, SSD_WIDTH + SSD_GROUPS * D_STATE:]

        dt = _softplus(misc_ref[pl.ds(off, t), :] + dtb_ref[...])
        acum = _dot_exact(ltri, dt * a_lane)
        acum_t = acum.T
        dt_e = _dot_exact(dt, expand)
        ac_e = _dot_exact(acum, expand)
        alast_e = ac_e[t - 1:t, :]
        xdt = xs * dt_e
        xw = (xdt * jnp.exp(alast_e - ac_e)).astype(BF16)
        eac = jnp.exp(ac_e)
        xdt_b = xdt.astype(BF16)

        ys = []
        for g in range(SSD_GROUPS):
            gs = slice(g * gw, (g + 1) * gw)
            bg = bm[:, g * D_STATE:(g + 1) * D_STATE]
            cg = cm[:, g * D_STATE:(g + 1) * D_STATE].astype(BF16)
            cb = _dot_nt(cg, bg.astype(BF16))
            st_old = st_scr[:, gs]
            y_off = _dot(cg, st_old.astype(BF16)) * eac[:, gs]
            st_scr[:, gs] = jnp.exp(alast_e[:, gs]) * st_old + _dot(bg.T.astype(BF16), xw[:, gs])
            for hh in range(hpg):
                h = g * hpg + hh
                col = acum[:, MISC_DT + h:MISC_DT + h + 1]
                row = acum_t[MISC_DT + h:MISC_DT + h + 1, :]
                decay = jnp.exp(jnp.where(lower, col - row, -jnp.inf))
                wmat = (cb * decay).astype(BF16)
                hs = slice(h * SSD_HEAD_DIM, (h + 1) * SSD_HEAD_DIM)
                ys.append(_dot(wmat, xdt_b[:, hs]) + y_off[:, hh * SSD_HEAD_DIM:(hh + 1) * SSD_HEAD_DIM])
        y = jnp.concatenate(ys, axis=1) + xs * dsk_ref[...]
        y = y * _silu(z_ref[pl.ds(off, t), :].astype(F32))
        outs = []
        for g in range(SSD_GROUPS):
            yg = y[:, g * gw:(g + 1) * gw]
            outs.append(yg * lax.rsqrt(jnp.mean(yg * yg, axis=-1, keepdims=True) + LN_EPS))
        o_ref[pl.ds(off, t), :] = (jnp.concatenate(outs, axis=1) * nw_ref[...]).astype(BF16)
        return carry

    lax.fori_loop(0, seq // t, chunk, 0)


def _ssd(z, xbc, misc, conv_w, conv_b, dtb_lane, alog_lane, dsk, norm_w, expand, bsz, seq):
    n = bsz * seq
    blk = lambda w: pl.BlockSpec((seq, w), lambda b: (b, 0))
    full = lambda a: pl.BlockSpec(a.shape, lambda b: (0, 0))
    return pl.pallas_call(
        _ssd_kernel,
        grid=(bsz,),
        in_specs=[blk(SSD_WIDTH), blk(XBC_WIDTH), blk(LANES), full(conv_w), full(conv_b),
                  full(dtb_lane), full(alog_lane), full(dsk), full(norm_w), full(expand)],
        out_specs=blk(SSD_WIDTH),
        out_shape=jax.ShapeDtypeStruct((n, SSD_WIDTH), BF16),
        scratch_shapes=[pltpu.VMEM((8, XBC_WIDTH), F32),
                        pltpu.VMEM((D_STATE, SSD_WIDTH), F32)],
        compiler_params=_cparams("parallel"),
        name="ssd_mixer",
    )(z, xbc, misc, conv_w, conv_b, dtb_lane, alog_lane, dsk, norm_w, expand)


def _outproj_kernel(attn_ref, ssd_ref, x_ref, wa_ref, ws_ref, g_ref, b_ref, h_ref, hb_ref, *, alpha):
    mix = _dot(attn_ref[...], wa_ref[...]) + _dot(ssd_ref[...], ws_ref[...])
    h = _layer_norm(alpha * x_ref[...] + mix, g_ref[...], b_ref[...])
    h_ref[...] = h
    hb_ref[...] = h.astype(BF16)


def _outproj(attn, ssd, x2, wa, ws, g, b, alpha, tm):
    n, d = x2.shape
    row = lambda w: pl.BlockSpec((tm, w), lambda i: (i, 0))
    full = lambda a: pl.BlockSpec(a.shape, lambda i: (0, 0))
    return pl.pallas_call(
        functools.partial(_outproj_kernel, alpha=alpha),
        grid=(n // tm,),
        in_specs=[row(ATTN_WIDTH), row(SSD_WIDTH), row(d), full(wa), full(ws), full(g), full(b)],
        out_specs=[row(d), row(d)],
        out_shape=[jax.ShapeDtypeStruct((n, d), F32), jax.ShapeDtypeStruct((n, d), BF16)],
        compiler_params=_cparams("parallel"),
        name="outproj_ln1",
    )(attn, ssd, x2, wa, ws, g, b)


def _peer_score_kernel(hb_ref, wq_ref, k1_ref, k2_ref, s1_ref, s2_ref):
    qb = _dot(hb_ref[...], wq_ref[...]).astype(BF16)
    k1 = k1_ref[...]
    k2 = k2_ref[...]
    for h in range(PEER_HEADS):
        o = h * 2 * PEER_HALF
        s1_ref[h] = _dot_nt(k1, qb[:, o:o + PEER_HALF])
        s2_ref[h] = _dot_nt(k2, qb[:, o + PEER_HALF:o + 2 * PEER_HALF])


def _peer_scores(hb, wq, k1, k2, tm):
    n, d = hb.shape
    full = lambda a: pl.BlockSpec(a.shape, lambda i: (0, 0))
    sblk = pl.BlockSpec((PEER_HEADS, N_KEYS, tm), lambda i: (0, 0, i))
    sshape = jax.ShapeDtypeStruct((PEER_HEADS, N_KEYS, n), F32)
    return pl.pallas_call(
        _peer_score_kernel,
        grid=(n // tm,),
        in_specs=[pl.BlockSpec((tm, d), lambda i: (i, 0)), full(wq), full(k1), full(k2)],
        out_specs=[sblk, sblk],
        out_shape=[sshape, sshape],
        compiler_params=_cparams("parallel"),
        name="peer_scores",
    )(hb, wq, k1, k2)


_CAND_COUNTS = [PEER_TOPK // (j1 + 1) for j1 in range(PEER_SUB_TOPK)]
_CAND_ROWS = sum(_CAND_COUNTS)
_CAND_PAD = -(-_CAND_ROWS // 8) * 8


def _peer_select_kernel(s1_ref, s2_ref, pos_ref, ea_ref, n1_ref, eb_ref, r2_ref, v_scr, cand_scr):
    tl = s1_ref.shape[2]
    kio = lax.broadcasted_iota(jnp.int32, (N_KEYS, tl), 0).astype(F32)
    cpos = jnp.broadcast_to(pos_ref[...], (_CAND_PAD, tl))

    def top_ranks(s, slot):
        rank = jnp.full((N_KEYS, tl), float(PEER_SUB_TOPK), F32)
        for jj in range(PEER_SUB_TOPK):
            m = jnp.max(s, axis=0, keepdims=True)
            idx = jnp.min(jnp.where(s == m, kio, float(N_KEYS)), axis=0, keepdims=True)
            hit = kio == idx
            s = jnp.where(hit, -jnp.inf, s)
            rank = jnp.where(hit, float(jj), rank)
            v_scr[slot, jj:jj + 1, :] = m
        return rank

    def head(h, carry):
        s1 = s1_ref[h]
        s2 = s2_ref[h]
        rank1 = top_ranks(s1, 0)
        rank2 = top_ranks(s2, 1)
        v1 = v_scr[0]
        v2 = v_scr[1]
        o = 0
        for j1, cnt in enumerate(_CAND_COUNTS):
            cand_scr[o:o + cnt, :] = v1[j1:j1 + 1, :] + v2[0:cnt, :]
            o += cnt
        if _CAND_PAD > _CAND_ROWS:
            cand_scr[_CAND_ROWS:, :] = jnp.full((_CAND_PAD - _CAND_ROWS, tl), -jnp.inf, F32)
        cand = cand_scr[...]
        picked = jnp.zeros((_CAND_PAD, tl), F32)
        zsum = jnp.zeros((1, tl), F32)
        best0 = None
        for kk in range(PEER_TOPK):
            m = jnp.max(cand, axis=0, keepdims=True)
            idx = jnp.min(jnp.where(cand == m, cpos, 1e9), axis=0, keepdims=True)
            hit = cpos == idx
            cand = jnp.where(hit, -jnp.inf, cand)
            picked = jnp.where(hit, 1.0, picked)
            if kk == 0:
                best0 = m
            zsum = zsum + jnp.exp(m - best0)
        n1 = jnp.zeros((N_KEYS, tl), F32)
        cand_scr[...] = picked
        o = 0
        for j1, cnt in enumerate(_CAND_COUNTS):
            nsel = jnp.sum(cand_scr[o:o + cnt, :], axis=0, keepdims=True)
            n1 = jnp.where(rank1 == float(j1), nsel, n1)
            o += cnt
        ea_ref[h] = jnp.exp(s1 - v1[0:1, :]) * (0.5 / zsum)
        n1_ref[h] = n1
        eb_ref[h] = jnp.exp(s2 - v2[0:1, :]).astype(BF16)
        r2_ref[h] = rank2.astype(BF16)
        return carry

    lax.fori_loop(0, PEER_HEADS, head, 0)


def _peer_select(s1t, s2t, cand_pos, tl):
    n = s1t.shape[2]
    blk = pl.BlockSpec((PEER_HEADS, N_KEYS, tl), lambda i: (0, 0, i))
    shp = lambda dt: jax.ShapeDtypeStruct((PEER_HEADS, N_KEYS, n), dt)
    return pl.pallas_call(
        _peer_select_kernel,
        grid=(n // tl,),
        in_specs=[blk, blk, pl.BlockSpec(cand_pos.shape, lambda i: (0, 0))],
        out_specs=[blk, blk, blk, blk],
        out_shape=[shp(F32), shp(F32), shp(BF16), shp(BF16)],
        scratch_shapes=[pltpu.VMEM((2, PEER_SUB_TOPK, tl), F32), pltpu.VMEM((_CAND_PAD, tl), F32)],
        compiler_params=_cparams("parallel"),
        name="peer_select",
    )(s1t, s2t, cand_pos)


PEER_SUB = 256
BF16_ROWS = 16


def _peer_expert_kernel(ht_ref, u_ref, vt_ref, ea_ref, n1_ref, eb_ref, r2_ref, o_ref,
                        act0_scr, act1_scr, acc_scr):
    e = pl.program_id(1)
    es = u_ref.shape[0]
    tl = ht_ref.shape[1]
    grp = N_KEYS // BF16_ROWS
    nsb = es // PEER_SUB

    @pl.when(e == 0)
    def _():
        acc_scr[...] = jnp.zeros(acc_scr.shape, F32)
        act1_scr[...] = jnp.zeros(act1_scr.shape, BF16)

    def rows16(ref, h, cc):
        return jnp.broadcast_to(ref[h, cc:cc + 1, :], (BF16_ROWS, tl)).astype(BF16)[None]

    def step(act_w, act_r):
        zero = jnp.zeros((), BF16)
        ht = ht_ref[...]

        def first_matmul(sb):
            return _dot(u_ref[sb * PEER_SUB:(sb + 1) * PEER_SUB, :], ht)

        def second_matmul(piece):
            rows = acc_scr.shape[0] // nsb
            rs = slice(piece * rows, (piece + 1) * rows)
            acc_scr[rs, :] += _dot(vt_ref[rs, :], act_r[...])

        def gate_block(sb, pt):
            for c2 in range(PEER_SUB // N_KEYS):
                cc = sb * (PEER_SUB // N_KEYS) + c2
                g = jnp.zeros((grp, BF16_ROWS, tl), BF16)
                for h in range(PEER_HEADS):
                    take = r2_ref[h].reshape(grp, BF16_ROWS, tl) < rows16(n1_ref, h, cc)
                    g = g + jnp.where(take, eb_ref[h].reshape(grp, BF16_ROWS, tl), zero) * rows16(ea_ref, h, cc)
                p = pt[c2 * N_KEYS:(c2 + 1) * N_KEYS, :]
                gelu2 = p * (1.0 + lax.erf(p * (2.0 ** -0.5)))
                act_w[cc * N_KEYS:(cc + 1) * N_KEYS, :] = gelu2.astype(BF16) * g.reshape(N_KEYS, tl)

        pt = first_matmul(0)
        for sb in range(nsb):
            pt_next = first_matmul(sb + 1) if sb + 1 < nsb else None
            gate_block(sb, pt)
            second_matmul(sb)
            pt = pt_next

    @pl.when(lax.rem(e, 2) == 0)
    def _():
        step(act0_scr, act1_scr)

    @pl.when(lax.rem(e, 2) == 1)
    def _():
        step(act1_scr, act0_scr)

    @pl.when(e == pl.num_programs(1) - 1)
    def _():
        o_ref[...] = acc_scr[...]


def _peer_experts(ht, u, vt, ea, n1, eb, r2, tl, es):
    d, n = ht.shape
    ng = u.shape[0] // es
    rows = es // N_KEYS
    cur_g = lambda e: jnp.minimum(e, ng - 1)
    a_blk = pl.BlockSpec((PEER_HEADS, rows, tl), lambda i, e: (0, cur_g(e), i))
    b_blk = pl.BlockSpec((PEER_HEADS, N_KEYS, tl), lambda i, e: (0, 0, i))
    return pl.pallas_call(
        _peer_expert_kernel,
        grid=(n // tl, ng + 1),
        in_specs=[pl.BlockSpec((d, tl), lambda i, e: (0, i)),
                  pl.BlockSpec((es, d), lambda i, e: (cur_g(e), 0)),
                  pl.BlockSpec((d, es), lambda i, e: (0, jnp.maximum(e - 1, 0))),
                  a_blk, a_blk, b_blk, b_blk],
        out_specs=pl.BlockSpec((d, tl), lambda i, e: (0, i)),
        out_shape=jax.ShapeDtypeStruct((d, n), F32),
        scratch_shapes=[pltpu.VMEM((es, tl), BF16), pltpu.VMEM((es, tl), BF16), pltpu.VMEM((d, tl), F32)],
        compiler_params=_cparams("parallel", "arbitrary"),
        name="peer_experts",
    )(ht, u, vt, ea, n1, eb, r2)


def _ln2_kernel(h_ref, f_ref, g_ref, b_ref, o_ref, *, alpha):
    o_ref[...] = _layer_norm(alpha * h_ref[...] + f_ref[...], g_ref[...], b_ref[...])


def _ln2(h, ffn, g, b, alpha, tm):
    n, d = h.shape
    row = pl.BlockSpec((tm, d), lambda i: (i, 0))
    full = lambda a: pl.BlockSpec(a.shape, lambda i: (0, 0))
    return pl.pallas_call(
        functools.partial(_ln2_kernel, alpha=alpha),
        grid=(n // tm,),
        in_specs=[row, row, full(g), full(b)],
        out_specs=row,
        out_shape=jax.ShapeDtypeStruct((n, d), F32),
        compiler_params=_cparams("parallel"),
        name="residual_ln2",
    )(h, ffn, g, b)


def _pick(n, pref):
    t = min(pref, n)
    while n % t:
        t //= 2
    return t


def _layer(x, positions, w_in, conv_w, conv_b, dt_bias, a_log, d_skip, ssm_norm_w, w_out,
           ln1_g, ln1_b, peer_wq, peer_k1, peer_k2, peer_u, peer_v, ln2_g, ln2_b, alpha):
    bsz, seq, d = x.shape
    n = bsz * seq
    x2 = x.reshape(n, d)
    pos2 = positions.reshape(n, 1)

    sizes = (ATTN_WIDTH, ATTN_WIDTH, ATTN_WIDTH, IDX_HEADS * IDX_DIM, IDX_DIM, IDX_HEADS,
             SSD_WIDTH, XBC_WIDTH, SSD_HEADS)
    offs = [0]
    for s in sizes:
        offs.append(offs[-1] + s)
    col = lambda i: w_in[:, offs[i]:offs[i + 1]]
    wv = jnp.pad(col(2).reshape(d, ATTN_HEADS, HEAD_DIM), ((0, 0), (0, 0), (0, LANES - HEAD_DIM)))
    w_main = jnp.concatenate([col(0), col(1), wv.reshape(d, ATTN_HEADS * LANES), col(3), col(6), col(7)],
                             axis=1).astype(BF16)
    w_misc = jnp.concatenate(
        [col(4), col(5), col(8), jnp.zeros((d, LANES - IDX_DIM - IDX_HEADS - SSD_HEADS), w_in.dtype)],
        axis=1).astype(BF16)
    lane = jnp.arange(LANES)
    inv64 = (ROPE_THETA ** (-jnp.arange(0, HEAD_DIM, 2, dtype=F32) / HEAD_DIM))[(lane % HEAD_DIM) % (HEAD_DIM // 2)]
    inv32 = (ROPE_THETA ** (-jnp.arange(0, IDX_DIM, 2, dtype=F32) / IDX_DIM))[(lane % IDX_DIM) % (IDX_DIM // 2)]

    q, k, va, iq, z, xbc, misc = _inproj(x2, pos2, w_main, w_misc, inv64[None, :], inv32[None, :], _pick(n, 512))

    attn = _dsa(q, k, va, iq, misc, bsz, seq, _pick(seq, 256))

    pad_dt = lambda a: jnp.zeros((1, LANES), F32).at[0, MISC_DT:MISC_DT + SSD_HEADS].set(a.astype(F32))
    expand = jnp.zeros((LANES, SSD_WIDTH), F32).at[
        MISC_DT + jnp.arange(SSD_WIDTH) // SSD_HEAD_DIM, jnp.arange(SSD_WIDTH)].set(1.0)
    ssd = _ssd(z, xbc, misc, conv_w.astype(F32), conv_b.astype(F32)[None, :], pad_dt(dt_bias), pad_dt(a_log),
               jnp.repeat(d_skip.astype(F32), SSD_HEAD_DIM)[None, :], ssm_norm_w.astype(F32)[None, :],
               expand, bsz, seq)

    wob = w_out.astype(BF16)
    h, hb = _outproj(attn, ssd, x2, wob[:ATTN_WIDTH], wob[ATTN_WIDTH:], ln1_g[None, :], ln1_b[None, :],
                     alpha, _pick(n, 512))

    s1t, s2t = _peer_scores(hb, peer_wq.astype(BF16), peer_k1.astype(BF16), peer_k2.astype(BF16), _pick(n, 512))

    cand_pos = []
    for j1, cnt in enumerate(_CAND_COUNTS):
        cand_pos += [float(j1 * PEER_SUB_TOPK + j2) for j2 in range(cnt)]
    cand_pos += [1e9] * (_CAND_PAD - _CAND_ROWS)
    ea, n1, eb, r2 = _peer_select(s1t, s2t, jnp.asarray(cand_pos, F32)[:, None], _pick(n, 256))

    ffn_t = _peer_experts(hb.T, peer_u.astype(BF16), peer_v.T.astype(BF16), ea, n1, eb, r2,
                          _pick(n, 512), 8 * N_KEYS)

    out = _ln2(h, ffn_t.T, ln2_g[None, :], ln2_b[None, :], alpha, _pick(n, 512))
    return out.reshape(bsz, seq, d)


def kernel(x, positions, w_in, conv_w, conv_b, dt_bias, a_log, d_skip, ssm_norm_w, w_out, ln1_g, ln1_b,
           peer_wq, peer_k1, peer_k2, peer_u, peer_v, ln2_g, ln2_b):
    depth = w_in.shape[0]
    alpha = float((2 * depth) ** 0.25)
    for i in range(depth):
        x = _layer(x, positions, w_in[i], conv_w[i], conv_b[i], dt_bias[i], a_log[i], d_skip[i],
                   ssm_norm_w[i], w_out[i], ln1_g[i], ln1_b[i], peer_wq[i], peer_k1[i], peer_k2[i],
                   peer_u[i], peer_v[i], ln2_g[i], ln2_b[i], alpha)
    return x
```

```python
import functools
import math

import jax
import jax.numpy as jnp
from jax import lax
from jax.experimental import pallas as pl
from jax.experimental.pallas import tpu as pltpu

F32 = jnp.float32
BF16 = jnp.bfloat16

ATTN_HEADS = 8
HEAD_DIM = 64
ATTN_WIDTH = ATTN_HEADS * HEAD_DIM
IDX_HEADS = 8
IDX_DIM = 32
MAX_TOPK_KEYS = 256
SSD_HEADS = 8
SSD_HEAD_DIM = 64
SSD_WIDTH = SSD_HEADS * SSD_HEAD_DIM
SSD_GROUPS = 2
D_STATE = 128
CONV_WIDTH = 4
CHUNK = 128
XBC_WIDTH = SSD_WIDTH + 2 * SSD_GROUPS * D_STATE
PEER_HEADS = 8
N_KEYS = 128
PEER_HALF = 128
PEER_SUB_TOPK = 16
PEER_TOPK = 16
ROPE_THETA = 10000.0
LN_EPS = 1e-5

MISC_IK = 0
MISC_IW = IDX_DIM
MISC_DT = IDX_DIM + IDX_HEADS
LANES = 128

F32_MIN_NORMAL = float(2.0 ** -126)
F32_MAX = float((2.0 - 2.0 ** -23) * 2.0 ** 127)
NEG_BIG = -1e30
ATTN_ROWS = 128

VMEM_LIMIT = 56 * 1024 * 1024


def _cparams(*sem):
    return pltpu.CompilerParams(dimension_semantics=sem, vmem_limit_bytes=VMEM_LIMIT)


def _dot(a, b):
    return jnp.dot(a, b, preferred_element_type=F32)


def _dot_nt(a, b):
    return lax.dot_general(a, b, (((1,), (1,)), ((), ())), preferred_element_type=F32)


def _dot_exact(a, b):
    return jnp.dot(a, b, preferred_element_type=F32, precision=lax.Precision.HIGHEST)


def _silu(x):
    return x / (1.0 + jnp.exp(-x))


def _softplus(x):
    return jnp.maximum(x, 0.0) + jnp.log(1.0 + jnp.exp(-jnp.abs(x)))


def _layer_norm(y, g, b):
    mu = jnp.mean(y, axis=-1, keepdims=True)
    d = y - mu
    var = jnp.mean(d * d, axis=-1, keepdims=True)
    return d * lax.rsqrt(var + LN_EPS) * g + b


def _swap_halves(x, half):
    w = x.shape[-1]
    lane = lax.broadcasted_iota(jnp.int32, x.shape, x.ndim - 1)
    first = (lane & (2 * half - 1)) < half
    return jnp.where(first, pltpu.roll(x, w - half, x.ndim - 1), pltpu.roll(x, half, x.ndim - 1))


def _inproj_kernel(x_ref, pos_ref, wm_ref, wx_ref, inv64_ref, inv32_ref,
                   q_ref, k_ref, va_ref, iq_ref, z_ref, xbc_ref, misc_ref):
    xb = x_ref[...].astype(BF16)
    pos = pos_ref[...].astype(F32)
    lane = lax.broadcasted_iota(jnp.int32, (1, LANES), 1)

    ang64 = pos * inv64_ref[...]
    cos64 = jnp.cos(ang64)
    sin64 = jnp.where((lane & (HEAD_DIM - 1)) < HEAD_DIM // 2, -1.0, 1.0) * jnp.sin(ang64)
    ang32 = pos * inv32_ref[...]
    cos32 = jnp.cos(ang32)
    sin32 = jnp.where((lane & (IDX_DIM - 1)) < IDX_DIM // 2, -1.0, 1.0) * jnp.sin(ang32)

    def rope(t, cos_t, sin_t, half):
        reps = t.shape[-1] // LANES
        c = jnp.concatenate([cos_t] * reps, axis=1) if reps > 1 else cos_t
        s = jnp.concatenate([sin_t] * reps, axis=1) if reps > 1 else sin_t
        return t * c + _swap_halves(t, half) * s

    o = 0
    q = _dot(xb, wm_ref[:, o:o + ATTN_WIDTH]); o += ATTN_WIDTH
    q_ref[...] = (rope(q, cos64, sin64, HEAD_DIM // 2) * (HEAD_DIM ** -0.5)).astype(BF16)
    k = _dot(xb, wm_ref[:, o:o + ATTN_WIDTH]); o += ATTN_WIDTH
    k_ref[...] = rope(k, cos64, sin64, HEAD_DIM // 2).astype(BF16)
    va = _dot(xb, wm_ref[:, o:o + ATTN_HEADS * LANES]); o += ATTN_HEADS * LANES
    pad = lax.broadcasted_iota(jnp.int32, (1, ATTN_HEADS * LANES), 1) & (LANES - 1)
    va_ref[...] = jnp.where(pad >= HEAD_DIM, 1.0, va).astype(BF16)
    iq = _dot(xb, wm_ref[:, o:o + IDX_HEADS * IDX_DIM]); o += IDX_HEADS * IDX_DIM
    iq_ref[...] = rope(iq, cos32, sin32, IDX_DIM // 2).astype(BF16)
    z_ref[...] = _dot(xb, wm_ref[:, o:o + SSD_WIDTH]).astype(BF16); o += SSD_WIDTH
    xbc_ref[...] = _dot(xb, wm_ref[:, o:o + XBC_WIDTH]).astype(BF16)
    misc = _dot(xb, wx_ref[...])
    misc_ref[...] = jnp.where(lane < IDX_DIM, rope(misc, cos32, sin32, IDX_DIM // 2), misc)


def _inproj(x2, pos2, w_main, w_misc, inv64, inv32, tm):
    n, d = x2.shape
    wm = w_main.shape[1]
    row = lambda w: pl.BlockSpec((tm, w), lambda i: (i, 0))
    full = lambda a: pl.BlockSpec(a.shape, lambda i: (0, 0))
    outs = [(ATTN_WIDTH, BF16), (ATTN_WIDTH, BF16), (ATTN_HEADS * LANES, BF16),
            (IDX_HEADS * IDX_DIM, BF16), (SSD_WIDTH, BF16), (XBC_WIDTH, BF16), (LANES, F32)]
    return pl.pallas_call(
        _inproj_kernel,
        grid=(n // tm,),
        in_specs=[row(d), row(1), full(w_main), full(w_misc), full(inv64), full(inv32)],
        out_specs=[row(w) for w, _ in outs],
        out_shape=[jax.ShapeDtypeStruct((n, w), dt) for w, dt in outs],
        compiler_params=_cparams("parallel"),
        name="inproj_rope",
    )(x2, pos2, w_main, w_misc, inv64, inv32)


def _dsa_kernel(q_ref, k_ref, va_ref, iq_ref, mq_ref, mk_ref, o_ref,
                s_scr, m_scr, acc_scr, lg_scr, p_scr, a_scr, *, topk, tq):
    tk = tq
    j = pl.program_id(1)
    nch = j + 1
    kf = float(topk)

    iw_t = mq_ref[...].T[MISC_IW:MISC_IW + IDX_HEADS, :] * (1.0 / 16.0)
    iq = iq_ref[...]
    qpos = j * tq + lax.broadcasted_iota(jnp.int32, (tk, tq), 1)
    kio = lax.broadcasted_iota(jnp.int32, (tk, tq), 0)

    def score_chunk(c, carry):
        off = pl.multiple_of(c * tk, tk)
        ikc = mk_ref[pl.ds(off, tk), MISC_IK:MISC_IK + IDX_DIM].astype(BF16)
        for h in range(IDX_HEADS):
            lg_scr[h] = _dot_nt(ikc, iq[:, h * IDX_DIM:(h + 1) * IDX_DIM])
        sc = jnp.zeros((tk, tq), F32)
        for h in range(IDX_HEADS):
            sc = sc + iw_t[h:h + 1, :] * jnp.maximum(lg_scr[h], 0.0)
        s_scr[c] = jnp.where(c * tk + kio <= qpos, sc, -jnp.inf)
        return carry

    lax.fori_loop(0, nch, score_chunk, 0)

    def count(pred):
        def body(c, acc):
            s3 = s_scr[c].reshape(tk // 8, 8, tq)
            return acc + jnp.sum(jnp.where(pred(s3, c), 1.0, 0.0), axis=0)
        acc = lax.fori_loop(0, nch, body, jnp.zeros((8, tq), F32))
        return jnp.sum(acc, axis=0, keepdims=True)

    def rows8(t):
        return jnp.broadcast_to(t, (8, tq))[None]

    def count_ge(t):
        t8 = rows8(t)
        return count(lambda s3, c: s3 >= t8)

    pos_row = count_ge(jnp.zeros((1, tq), F32)) >= kf
    sgn = jnp.where(pos_row, 1.0, -1.0)

    def accept(trial_mag):
        ok = jnp.where(count_ge(sgn * trial_mag) >= kf, 1.0, -1.0)
        return ok == sgn

    mag = jnp.where(accept(jnp.full((1, tq), F32_MIN_NORMAL, F32)), F32_MIN_NORMAL, 0.0)
    for b in range(7, -1, -1):
        if b == 7:
            trial = (mag * float(2.0 ** 64)) * float(2.0 ** 64)
        else:
            trial = mag * float(2.0 ** (2 ** b))
        trial = jnp.minimum(trial, F32_MAX)
        mag = jnp.where(accept(trial), trial, mag)
    base = mag
    for kbit in range(1, 24):
        trial = jnp.minimum(mag + base * float(2.0 ** -kbit), F32_MAX)
        mag = jnp.where(accept(trial), trial, mag)
    ulp = jnp.where(base > 0.0, base * float(2.0 ** -23), F32_MIN_NORMAL)
    thr = jnp.where(pos_row, mag, -(mag + ulp))

    n_ge = count_ge(thr)

    @pl.when(jnp.max(n_ge) > kf)
    def _():
        t8 = rows8(thr)
        kio3 = kio.reshape(tk // 8, 8, tq)
        n_gt = count(lambda s3, c: s3 > t8)
        n_eq = n_ge - n_gt
        target = jnp.minimum(kf - n_gt, n_eq)
        mpos = jnp.zeros((1, tq), F32)
        nbits = int(math.ceil(math.log2(tk * s_scr.shape[0]))) + 1
        for b in range(nbits - 1, -1, -1):
            trial = mpos + float(2 ** b)
            tr8 = rows8(trial)
            cnt = count(lambda s3, c: jnp.where(s3 == t8, (c * tk + kio3).astype(F32), 1e9) < tr8)
            mpos = jnp.where(cnt < target, trial, mpos)

        def drop(c, carry):
            s = s_scr[c]
            kill = jnp.where(s == thr, (c * tk + kio).astype(F32), -1.0) > mpos
            s_scr[c] = jnp.where(kill, -jnp.inf, s)
            return carry

        lax.fori_loop(0, nch, drop, 0)

    def wide(t):
        return jnp.concatenate([t] * (tk // LANES), axis=1)

    thr_q = jnp.broadcast_to(jnp.maximum(thr, -F32_MAX), (LANES, tq)).T
    thr_w = wide(thr_q)
    m_scr[...] = jnp.full(m_scr.shape, NEG_BIG, F32)
    acc_scr[...] = jnp.zeros(acc_scr.shape, F32)

    def attn_chunk(c, carry):
        off = pl.multiple_of(c * tk, tk)
        sel = s_scr[c].T >= thr_w
        for h in range(ATTN_HEADS):
            hs = slice(h * HEAD_DIM, (h + 1) * HEAD_DIM)
            lg_scr[h] = _dot_nt(q_ref[:, hs], k_ref[pl.ds(off, tk), hs])
        for h in range(ATTN_HEADS):
            lg = jnp.where(sel, lg_scr[h], -jnp.inf)
            m_old = m_scr[h]
            m_new = jnp.maximum(m_old, jnp.max(lg, axis=1, keepdims=True))
            p_scr[h] = jnp.exp(lg - wide(m_new)).astype(BF16)
            a_scr[h] = jnp.exp(m_old - m_new)
            m_scr[h] = m_new
        for h in range(ATTN_HEADS):
            pv = _dot(p_scr[h], va_ref[pl.ds(off, tk), h * LANES:(h + 1) * LANES])
            acc_scr[h] = a_scr[h] * acc_scr[h] + pv
        return carry

    lax.fori_loop(0, nch, attn_chunk, 0)
    for h in range(ATTN_HEADS):
        a = acc_scr[h]
        o_ref[:, h * HEAD_DIM:(h + 1) * HEAD_DIM] = (a[:, :HEAD_DIM] / a[:, HEAD_DIM:]).astype(BF16)


def _dsa(q, k, va, iq, misc, bsz, seq, tq):
    n = bsz * seq
    nq = seq // tq
    topk = min(MAX_TOPK_KEYS, seq // 4)
    qblk = lambda w: pl.BlockSpec((tq, w), lambda b, j: (b * nq + j, 0))
    kblk = lambda w: pl.BlockSpec((seq, w), lambda b, j: (b, 0))
    return pl.pallas_call(
        functools.partial(_dsa_kernel, topk=topk, tq=tq),
        grid=(bsz, nq),
        in_specs=[qblk(ATTN_WIDTH), kblk(ATTN_WIDTH), kblk(ATTN_HEADS * LANES),
                  qblk(IDX_HEADS * IDX_DIM), qblk(LANES), kblk(LANES)],
        out_specs=qblk(ATTN_WIDTH),
        out_shape=jax.ShapeDtypeStruct((n, ATTN_WIDTH), BF16),
        scratch_shapes=[
            pltpu.VMEM((nq, tq, tq), F32),
            pltpu.VMEM((ATTN_HEADS, tq, LANES), F32),
            pltpu.VMEM((ATTN_HEADS, tq, LANES), F32),
            pltpu.VMEM((ATTN_HEADS, tq, tq), F32),
            pltpu.VMEM((ATTN_HEADS, tq, tq), BF16),
            pltpu.VMEM((ATTN_HEADS, tq, LANES), F32),
        ],
        compiler_params=_cparams("parallel", "arbitrary"),
        name="dsa_attention",
    )(q, k, va, iq, misc, misc)


def _ssd_kernel(z_ref, xbc_ref, misc_ref, cw_ref, cb_ref, dtb_ref, alog_ref, dsk_ref, nw_ref, e_ref,
                o_ref, tail_scr, st_scr):
    seq = z_ref.shape[0]
    t = CHUNK
    tail_scr[...] = jnp.zeros(tail_scr.shape, F32)
    st_scr[...] = jnp.zeros(st_scr.shape, F32)
    a_lane = -jnp.exp(alog_ref[...])
    r = lax.broadcasted_iota(jnp.int32, (t, t), 0)
    cidx = lax.broadcasted_iota(jnp.int32, (t, t), 1)
    lower = r >= cidx
    ltri = jnp.where(lower, 1.0, 0.0)
    expand = e_ref[...]
    gw = SSD_WIDTH // SSD_GROUPS
    hpg = SSD_HEADS // SSD_GROUPS

    def chunk(c, carry):
        off = pl.multiple_of(c * t, t)
        xin = xbc_ref[pl.ds(off, t), :].astype(F32)
        xp = jnp.concatenate([tail_scr[...], xin], axis=0)
        tail_scr[...] = xin[t - 8:, :]
        acc = jnp.broadcast_to(cb_ref[...], (t, XBC_WIDTH))
        for w in range(CONV_WIDTH):
            s0 = 8 - (CONV_WIDTH - 1) + w
            acc = acc + xp[s0:s0 + t, :] * cw_ref[w:w + 1, :]
        xc = _silu(acc)
        xs = xc[:, :SSD_WIDTH]
        bm = xc[:, SSD_WIDTH:SSD_WIDTH + SSD_GROUPS * D_STATE]
        cm = xc[:, SSD_WIDTH + SSD_GROUPS * D_STATE:]

        dt = _softplus(misc_ref[pl.ds(off, t), :] + dtb_ref[...])
        acum = _dot_exact(ltri, dt * a_lane)
        acum_t = acum.T
        dt_e = _dot_exact(dt, expand)
        ac_e = _dot_exact(acum, expand)
        alast_e = ac_e[t - 1:t, :]
        xdt = xs * dt_e
        xw = (xdt * jnp.exp(alast_e - ac_e)).astype(BF16)
        eac = jnp.exp(ac_e)
        xdt_b = xdt.astype(BF16)

        ys = []
        for g in range(SSD_GROUPS):
            gs = slice(g * gw, (g + 1) * gw)
            bg = bm[:, g * D_STATE:(g + 1) * D_STATE]
            cg = cm[:, g * D_STATE:(g + 1) * D_STATE].astype(BF16)
            cb = _dot_nt(cg, bg.astype(BF16))
            st_old = st_scr[:, gs]
            y_off = _dot(cg, st_old.astype(BF16)) * eac[:, gs]
            st_scr[:, gs] = jnp.exp(alast_e[:, gs]) * st_old + _dot(bg.T.astype(BF16), xw[:, gs])
            for hh in range(hpg):
                h = g * hpg + hh
                col = acum[:, MISC_DT + h:MISC_DT + h + 1]
                row = acum_t[MISC_DT + h:MISC_DT + h + 1, :]
                decay = jnp.exp(jnp.where(lower, col - row, -jnp.inf))
                wmat = (cb * decay).astype(BF16)
                hs = slice(h * SSD_HEAD_DIM, (h + 1) * SSD_HEAD_DIM)
                ys.append(_dot(wmat, xdt_b[:, hs]) + y_off[:, hh * SSD_HEAD_DIM:(hh + 1) * SSD_HEAD_DIM])
        y = jnp.concatenate(ys, axis=1) + xs * dsk_ref[...]
        y = y * _silu(z_ref[pl.ds(off, t), :].astype(F32))
        outs = []
        for g in range(SSD_GROUPS):
            yg = y[:, g * gw:(g + 1) * gw]
            outs.append(yg * lax.rsqrt(jnp.mean(yg * yg, axis=-1, keepdims=True) + LN_EPS))
        o_ref[pl.ds(off, t), :] = (jnp.concatenate(outs, axis=1) * nw_ref[...]).astype(BF16)
        return carry

    lax.fori_loop(0, seq // t, chunk, 0)


def _ssd(z, xbc, misc, conv_w, conv_b, dtb_lane, alog_lane, dsk, norm_w, expand, bsz, seq):
    n = bsz * seq
    blk = lambda w: pl.BlockSpec((seq, w), lambda b: (b, 0))
    full = lambda a: pl.BlockSpec(a.shape, lambda b: (0, 0))
    return pl.pallas_call(
        _ssd_kernel,
        grid=(bsz,),
        in_specs=[blk(SSD_WIDTH), blk(XBC_WIDTH), blk(LANES), full(conv_w), full(conv_b),
                  full(dtb_lane), full(alog_lane), full(dsk), full(norm_w), full(expand)],
        out_specs=blk(SSD_WIDTH),
        out_shape=jax.ShapeDtypeStruct((n, SSD_WIDTH), BF16),
        scratch_shapes=[pltpu.VMEM((8, XBC_WIDTH), F32),
                        pltpu.VMEM((D_STATE, SSD_WIDTH), F32)],
        compiler_params=_cparams("parallel"),
        name="ssd_mixer",
    )(z, xbc, misc, conv_w, conv_b, dtb_lane, alog_lane, dsk, norm_w, expand)


def _outproj_kernel(attn_ref, ssd_ref, x_ref, wa_ref, ws_ref, g_ref, b_ref, h_ref, hb_ref, hbt_ref, *, alpha):
    mix = _dot(attn_ref[...], wa_ref[...]) + _dot(ssd_ref[...], ws_ref[...])
    h = _layer_norm(alpha * x_ref[...] + mix, g_ref[...], b_ref[...])
    h_ref[...] = h
    hb_ref[...] = h.astype(BF16)
    hbt_ref[...] = h.T.astype(BF16)


def _outproj(attn, ssd, x2, wa, ws, g, b, alpha, tm):
    n, d = x2.shape
    row = lambda w: pl.BlockSpec((tm, w), lambda i: (i, 0))
    full = lambda a: pl.BlockSpec(a.shape, lambda i: (0, 0))
    return pl.pallas_call(
        functools.partial(_outproj_kernel, alpha=alpha),
        grid=(n // tm,),
        in_specs=[row(ATTN_WIDTH), row(SSD_WIDTH), row(d), full(wa), full(ws), full(g), full(b)],
        out_specs=[row(d), row(d), pl.BlockSpec((d, tm), lambda i: (0, i))],
        out_shape=[jax.ShapeDtypeStruct((n, d), F32), jax.ShapeDtypeStruct((n, d), BF16),
                   jax.ShapeDtypeStruct((d, n), BF16)],
        compiler_params=_cparams("parallel"),
        name="outproj_ln1",
    )(attn, ssd, x2, wa, ws, g, b)


def _peer_score_kernel(hb_ref, wq_ref, k1_ref, k2_ref, s1_ref, s2_ref):
    qb = _dot(hb_ref[...], wq_ref[...]).astype(BF16)
    k1 = k1_ref[...]
    k2 = k2_ref[...]
    for h in range(PEER_HEADS):
        o = h * 2 * PEER_HALF
        s1_ref[h] = _dot_nt(k1, qb[:, o:o + PEER_HALF])
        s2_ref[h] = _dot_nt(k2, qb[:, o + PEER_HALF:o + 2 * PEER_HALF])


def _peer_scores(hb, wq, k1, k2, tm):
    n, d = hb.shape
    full = lambda a: pl.BlockSpec(a.shape, lambda i: (0, 0))
    sblk = pl.BlockSpec((PEER_HEADS, N_KEYS, tm), lambda i: (0, 0, i))
    sshape = jax.ShapeDtypeStruct((PEER_HEADS, N_KEYS, n), F32)
    return pl.pallas_call(
        _peer_score_kernel,
        grid=(n // tm,),
        in_specs=[pl.BlockSpec((tm, d), lambda i: (i, 0)), full(wq), full(k1), full(k2)],
        out_specs=[sblk, sblk],
        out_shape=[sshape, sshape],
        compiler_params=_cparams("parallel"),
        name="peer_scores",
    )(hb, wq, k1, k2)


_CAND_COUNTS = [PEER_TOPK // (j1 + 1) for j1 in range(PEER_SUB_TOPK)]
_CAND_ROWS = sum(_CAND_COUNTS)
_CAND_PAD = -(-_CAND_ROWS // 8) * 8


def _peer_select_kernel(s1_ref, s2_ref, pos_ref, ea_ref, n1_ref, eb_ref, r2_ref,
                        v_scr, rank_scr, cand_scr, pick_scr):
    tl = s1_ref.shape[2]
    kio = lax.broadcasted_iota(jnp.int32, (N_KEYS, tl), 0).astype(F32)
    cpos = jnp.broadcast_to(pos_ref[...], (_CAND_PAD, tl))
    not_ranked = float(PEER_SUB_TOPK)

    def top_ranks(s, slot, exact):
        rank = jnp.full((N_KEYS, tl), not_ranked, F32)
        for jj in range(PEER_SUB_TOPK):
            m = jnp.max(s, axis=0, keepdims=True)
            hit = s == m
            if exact:
                hit = kio == jnp.min(jnp.where(hit, kio, float(N_KEYS)), axis=0, keepdims=True)
            s = jnp.where(hit, -jnp.inf, s)
            rank = jnp.where(hit, float(jj), rank)
            v_scr[slot, jj:jj + 1, :] = m
        rank_scr[slot] = rank
        return jnp.max(jnp.sum(jnp.where(rank < not_ranked, 1.0, 0.0), axis=0, keepdims=True))

    def pick_pairs(exact):
        cand = cand_scr[...]
        picked = jnp.zeros((_CAND_PAD, tl), F32)
        for kk in range(PEER_TOPK):
            m = jnp.max(cand, axis=0, keepdims=True)
            hit = cand == m
            if exact:
                hit = cpos == jnp.min(jnp.where(hit, cpos, 1e9), axis=0, keepdims=True)
            cand = jnp.where(hit, -jnp.inf, cand)
            picked = jnp.where(hit, 1.0, picked)
        pick_scr[...] = picked
        return jnp.max(jnp.sum(picked, axis=0, keepdims=True))

    def head(h, carry):
        s1 = s1_ref[h]
        s2 = s2_ref[h]
        marked = jnp.maximum(top_ranks(s1, 0, False), top_ranks(s2, 1, False))

        @pl.when(marked > float(PEER_SUB_TOPK))
        def _():
            top_ranks(s1, 0, True)
            top_ranks(s2, 1, True)

        v1 = v_scr[0]
        v2 = v_scr[1]
        o = 0
        for j1, cnt in enumerate(_CAND_COUNTS):
            cand_scr[o:o + cnt, :] = v1[j1:j1 + 1, :] + v2[0:cnt, :]
            o += cnt
        if _CAND_PAD > _CAND_ROWS:
            cand_scr[_CAND_ROWS:, :] = jnp.full((_CAND_PAD - _CAND_ROWS, tl), -jnp.inf, F32)
        npicked = pick_pairs(False)

        @pl.when(npicked > float(PEER_TOPK))
        def _():
            pick_pairs(True)

        picked = pick_scr[...]
        best0 = v1[0:1, :] + v2[0:1, :]
        zsum = jnp.sum(jnp.where(picked > 0.0, jnp.exp(cand_scr[...] - best0), 0.0), axis=0, keepdims=True)
        rank1 = rank_scr[0]
        n1 = jnp.zeros((N_KEYS, tl), F32)
        o = 0
        for j1, cnt in enumerate(_CAND_COUNTS):
            nsel = jnp.sum(pick_scr[o:o + cnt, :], axis=0, keepdims=True)
            n1 = jnp.where(rank1 == float(j1), nsel, n1)
            o += cnt
        ea_ref[h] = jnp.exp(s1 - v1[0:1, :]) * (0.5 / zsum)
        n1_ref[h] = n1
        eb_ref[h] = jnp.exp(s2 - v2[0:1, :]).astype(BF16)
        r2_ref[h] = rank_scr[1].astype(BF16)
        return carry

    lax.fori_loop(0, PEER_HEADS, head, 0)


def _peer_select(s1t, s2t, cand_pos, tl):
    n = s1t.shape[2]
    blk = pl.BlockSpec((PEER_HEADS, N_KEYS, tl), lambda i: (0, 0, i))
    shp = lambda dt: jax.ShapeDtypeStruct((PEER_HEADS, N_KEYS, n), dt)
    return pl.pallas_call(
        _peer_select_kernel,
        grid=(n // tl,),
        in_specs=[blk, blk, pl.BlockSpec(cand_pos.shape, lambda i: (0, 0))],
        out_specs=[blk, blk, blk, blk],
        out_shape=[shp(F32), shp(F32), shp(BF16), shp(BF16)],
        scratch_shapes=[pltpu.VMEM((2, PEER_SUB_TOPK, tl), F32),
                        pltpu.VMEM((2, N_KEYS, tl), F32),
                        pltpu.VMEM((_CAND_PAD, tl), F32),
                        pltpu.VMEM((_CAND_PAD, tl), F32)],
        compiler_params=_cparams("parallel"),
        name="peer_select",
    )(s1t, s2t, cand_pos)


PEER_TOK = 256
BF16_ROWS = 16


def _peer_expert_kernel(ht_ref, u_ref, vt_ref, ea_ref, n1_ref, eb_ref, r2_ref, o_ref,
                        act0_scr, act1_scr, acc_scr, pt0_scr, pt1_scr):
    e = pl.program_id(1)
    es = u_ref.shape[0]
    tl = ht_ref.shape[1]
    grp = N_KEYS // BF16_ROWS

    @pl.when(e == 0)
    def _():
        acc_scr[...] = jnp.zeros(acc_scr.shape, F32)
        act1_scr[...] = jnp.zeros(act1_scr.shape, BF16)

    def step(act_w, act_r):
        zero = jnp.zeros((), BF16)
        npc = tl // PEER_TOK

        pt_bufs = (pt0_scr, pt1_scr)

        def first_matmul(p):
            pt_bufs[p % 2][...] = _dot(u_ref[...], ht_ref[:, p * PEER_TOK:(p + 1) * PEER_TOK])

        def second_matmul(p):
            cols = slice(p * PEER_TOK, (p + 1) * PEER_TOK)
            acc_scr[:, cols] += _dot(vt_ref[...], act_r[:, cols])

        def gate_block(p):
            cols = slice(p * PEER_TOK, (p + 1) * PEER_TOK)
            pt = pt_bufs[p % 2]

            def rows16(ref, h, cc):
                return jnp.broadcast_to(ref[h, cc:cc + 1, cols], (BF16_ROWS, PEER_TOK)).astype(BF16)[None]

            for cc in range(es // N_KEYS):
                g = jnp.zeros((grp, BF16_ROWS, PEER_TOK), BF16)
                for h in range(PEER_HEADS):
                    r2 = r2_ref[h, :, cols].reshape(grp, BF16_ROWS, PEER_TOK)
                    eb = eb_ref[h, :, cols].reshape(grp, BF16_ROWS, PEER_TOK)
                    g = g + jnp.where(r2 < rows16(n1_ref, h, cc), eb, zero) * rows16(ea_ref, h, cc)
                x = pt[cc * N_KEYS:(cc + 1) * N_KEYS, :]
                gelu2 = x * (1.0 + lax.erf(x * (2.0 ** -0.5)))
                act_w[cc * N_KEYS:(cc + 1) * N_KEYS, cols] = gelu2.astype(BF16) * g.reshape(N_KEYS, PEER_TOK)

        first_matmul(0)
        for p in range(npc):
            if p + 1 < npc:
                first_matmul(p + 1)
            second_matmul(p)
            gate_block(p)

    @pl.when(lax.rem(e, 2) == 0)
    def _():
        step(act0_scr, act1_scr)

    @pl.when(lax.rem(e, 2) == 1)
    def _():
        step(act1_scr, act0_scr)

    @pl.when(e == pl.num_programs(1) - 1)
    def _():
        o_ref[...] = acc_scr[...].T


def _peer_experts(ht, u, vt, ea, n1, eb, r2, tl, es):
    d, n = ht.shape
    ng = u.shape[0] // es
    rows = es // N_KEYS
    cur_g = lambda e: jnp.minimum(e, ng - 1)
    a_blk = pl.BlockSpec((PEER_HEADS, rows, tl), lambda i, e: (0, cur_g(e), i))
    b_blk = pl.BlockSpec((PEER_HEADS, N_KEYS, tl), lambda i, e: (0, 0, i))
    return pl.pallas_call(
        _peer_expert_kernel,
        grid=(n // tl, ng + 1),
        in_specs=[pl.BlockSpec((d, tl), lambda i, e: (0, i)),
                  pl.BlockSpec((es, d), lambda i, e: (cur_g(e), 0)),
                  pl.BlockSpec((d, es), lambda i, e: (0, jnp.maximum(e - 1, 0))),
                  a_blk, a_blk, b_blk, b_blk],
        out_specs=pl.BlockSpec((tl, d), lambda i, e: (i, 0)),
        out_shape=jax.ShapeDtypeStruct((n, d), F32),
        scratch_shapes=[pltpu.VMEM((es, tl), BF16), pltpu.VMEM((es, tl), BF16), pltpu.VMEM((d, tl), F32),
                        pltpu.VMEM((es, PEER_TOK), F32), pltpu.VMEM((es, PEER_TOK), F32)],
        compiler_params=_cparams("parallel", "arbitrary"),
        name="peer_experts",
    )(ht, u, vt, ea, n1, eb, r2)


def _ln2_kernel(h_ref, f_ref, g_ref, b_ref, o_ref, *, alpha):
    o_ref[...] = _layer_norm(alpha * h_ref[...] + f_ref[...], g_ref[...], b_ref[...])


def _ln2(h, ffn, g, b, alpha, tm):
    n, d = h.shape
    row = pl.BlockSpec((tm, d), lambda i: (i, 0))
    full = lambda a: pl.BlockSpec(a.shape, lambda i: (0, 0))
    return pl.pallas_call(
        functools.partial(_ln2_kernel, alpha=alpha),
        grid=(n // tm,),
        in_specs=[row, row, full(g), full(b)],
        out_specs=row,
        out_shape=jax.ShapeDtypeStruct((n, d), F32),
        compiler_params=_cparams("parallel"),
        name="residual_ln2",
    )(h, ffn, g, b)


def _pick(n, pref):
    t = min(pref, n)
    while n % t:
        t //= 2
    return t


def _layer(x, positions, w_in, conv_w, conv_b, dt_bias, a_log, d_skip, ssm_norm_w, w_out,
           ln1_g, ln1_b, peer_wq, peer_k1, peer_k2, peer_u, peer_v, ln2_g, ln2_b, alpha):
    bsz, seq, d = x.shape
    n = bsz * seq
    x2 = x.reshape(n, d)
    pos2 = positions.reshape(n, 1)

    sizes = (ATTN_WIDTH, ATTN_WIDTH, ATTN_WIDTH, IDX_HEADS * IDX_DIM, IDX_DIM, IDX_HEADS,
             SSD_WIDTH, XBC_WIDTH, SSD_HEADS)
    offs = [0]
    for s in sizes:
        offs.append(offs[-1] + s)
    col = lambda i: w_in[:, offs[i]:offs[i + 1]]
    wv = jnp.pad(col(2).reshape(d, ATTN_HEADS, HEAD_DIM), ((0, 0), (0, 0), (0, LANES - HEAD_DIM)))
    w_main = jnp.concatenate([col(0), col(1), wv.reshape(d, ATTN_HEADS * LANES), col(3), col(6), col(7)],
                             axis=1).astype(BF16)
    w_misc = jnp.concatenate(
        [col(4), col(5), col(8), jnp.zeros((d, LANES - IDX_DIM - IDX_HEADS - SSD_HEADS), w_in.dtype)],
        axis=1).astype(BF16)
    lane = jnp.arange(LANES)
    inv64 = (ROPE_THETA ** (-jnp.arange(0, HEAD_DIM, 2, dtype=F32) / HEAD_DIM))[(lane % HEAD_DIM) % (HEAD_DIM // 2)]
    inv32 = (ROPE_THETA ** (-jnp.arange(0, IDX_DIM, 2, dtype=F32) / IDX_DIM))[(lane % IDX_DIM) % (IDX_DIM // 2)]

    q, k, va, iq, z, xbc, misc = _inproj(x2, pos2, w_main, w_misc, inv64[None, :], inv32[None, :], _pick(n, 512))

    attn = _dsa(q, k, va, iq, misc, bsz, seq, _pick(seq, 256))

    pad_dt = lambda a: jnp.zeros((1, LANES), F32).at[0, MISC_DT:MISC_DT + SSD_HEADS].set(a.astype(F32))
    expand = (jnp.arange(LANES)[:, None] == MISC_DT + jnp.arange(SSD_WIDTH)[None, :] // SSD_HEAD_DIM).astype(F32)
    ssd = _ssd(z, xbc, misc, conv_w.astype(F32), conv_b.astype(F32)[None, :], pad_dt(dt_bias), pad_dt(a_log),
               jnp.repeat(d_skip.astype(F32), SSD_HEAD_DIM)[None, :], ssm_norm_w.astype(F32)[None, :],
               expand, bsz, seq)

    wob = w_out.astype(BF16)
    h, hb, hbt = _outproj(attn, ssd, x2, wob[:ATTN_WIDTH], wob[ATTN_WIDTH:], ln1_g[None, :], ln1_b[None, :],
                          alpha, _pick(n, 512))

    s1t, s2t = _peer_scores(hb, peer_wq.astype(BF16), peer_k1.astype(BF16), peer_k2.astype(BF16), _pick(n, 512))

    cand_pos = []
    for j1, cnt in enumerate(_CAND_COUNTS):
        cand_pos += [float(j1 * PEER_SUB_TOPK + j2) for j2 in range(cnt)]
    cand_pos += [1e9] * (_CAND_PAD - _CAND_ROWS)
    ea, n1, eb, r2 = _peer_select(s1t, s2t, jnp.asarray(cand_pos, F32)[:, None], _pick(n, 256))

    ffn = _peer_experts(hbt, peer_u.astype(BF16), peer_v.T.astype(BF16), ea, n1, eb, r2,
                        _pick(n, 1024), 8 * N_KEYS)

    out = _ln2(h, ffn, ln2_g[None, :], ln2_b[None, :], alpha, _pick(n, 512))
    return out.reshape(bsz, seq, d)


def kernel(x, positions, w_in, conv_w, conv_b, dt_bias, a_log, d_skip, ssm_norm_w, w_out, ln1_g, ln1_b,
           peer_wq, peer_k1, peer_k2, peer_u, peer_v, ln2_g, ln2_b):
    depth = w_in.shape[0]
    alpha = float((2 * depth) ** 0.25)
    for i in range(depth):
        x = _layer(x, positions, w_in[i], conv_w[i], conv_b[i], dt_bias[i], a_log[i], d_skip[i],
                   ssm_norm_w[i], w_out[i], ln1_g[i], ln1_b[i], peer_wq[i], peer_k1[i], peer_k2[i],
                   peer_u[i], peer_v[i], ln2_g[i], ln2_b[i], alpha)
    return x
```

```python
import functools
import math

import jax
import jax.numpy as jnp
from jax import lax
from jax.experimental import pallas as pl
from jax.experimental.pallas import tpu as pltpu

F32 = jnp.float32
BF16 = jnp.bfloat16

ATTN_HEADS = 8
HEAD_DIM = 64
ATTN_WIDTH = ATTN_HEADS * HEAD_DIM
IDX_HEADS = 8
IDX_DIM = 32
MAX_TOPK_KEYS = 256
SSD_HEADS = 8
SSD_HEAD_DIM = 64
SSD_WIDTH = SSD_HEADS * SSD_HEAD_DIM
SSD_GROUPS = 2
D_STATE = 128
CONV_WIDTH = 4
CHUNK = 128
XBC_WIDTH = SSD_WIDTH + 2 * SSD_GROUPS * D_STATE
PEER_HEADS = 8
N_KEYS = 128
PEER_HALF = 128
PEER_SUB_TOPK = 16
PEER_TOPK = 16
ROPE_THETA = 10000.0
LN_EPS = 1e-5

MISC_IK = 0
MISC_IW = IDX_DIM
MISC_DT = IDX_DIM + IDX_HEADS
LANES = 128

F32_MIN_NORMAL = float(2.0 ** -126)
F32_MAX = float((2.0 - 2.0 ** -23) * 2.0 ** 127)
NEG_BIG = -1e30
ATTN_ROWS = 128

VMEM_LIMIT = 56 * 1024 * 1024


def _cparams(*sem):
    return pltpu.CompilerParams(dimension_semantics=sem, vmem_limit_bytes=VMEM_LIMIT)


def _dot(a, b):
    return jnp.dot(a, b, preferred_element_type=F32)


def _dot_nt(a, b):
    return lax.dot_general(a, b, (((1,), (1,)), ((), ())), preferred_element_type=F32)


def _dot_exact(a, b):
    return jnp.dot(a, b, preferred_element_type=F32, precision=lax.Precision.HIGHEST)


def _silu(x):
    return x / (1.0 + jnp.exp(-x))


def _softplus(x):
    return jnp.maximum(x, 0.0) + jnp.log(1.0 + jnp.exp(-jnp.abs(x)))


def _layer_norm(y, g, b):
    mu = jnp.mean(y, axis=-1, keepdims=True)
    d = y - mu
    var = jnp.mean(d * d, axis=-1, keepdims=True)
    return d * lax.rsqrt(var + LN_EPS) * g + b


def _swap_halves(x, half):
    w = x.shape[-1]
    lane = lax.broadcasted_iota(jnp.int32, x.shape, x.ndim - 1)
    first = (lane & (2 * half - 1)) < half
    return jnp.where(first, pltpu.roll(x, w - half, x.ndim - 1), pltpu.roll(x, half, x.ndim - 1))


def _inproj_kernel(x_ref, pos_ref, wm_ref, wx_ref, inv64_ref, inv32_ref,
                   q_ref, k_ref, va_ref, iq_ref, z_ref, xbc_ref, misc_ref):
    xb = x_ref[...].astype(BF16)
    pos = pos_ref[...].astype(F32)
    lane = lax.broadcasted_iota(jnp.int32, (1, LANES), 1)

    ang64 = pos * inv64_ref[...]
    cos64 = jnp.cos(ang64)
    sin64 = jnp.where((lane & (HEAD_DIM - 1)) < HEAD_DIM // 2, -1.0, 1.0) * jnp.sin(ang64)
    ang32 = pos * inv32_ref[...]
    cos32 = jnp.cos(ang32)
    sin32 = jnp.where((lane & (IDX_DIM - 1)) < IDX_DIM // 2, -1.0, 1.0) * jnp.sin(ang32)

    def rope(t, cos_t, sin_t, half):
        reps = t.shape[-1] // LANES
        c = jnp.concatenate([cos_t] * reps, axis=1) if reps > 1 else cos_t
        s = jnp.concatenate([sin_t] * reps, axis=1) if reps > 1 else sin_t
        return t * c + _swap_halves(t, half) * s

    o = 0
    q = _dot(xb, wm_ref[:, o:o + ATTN_WIDTH]); o += ATTN_WIDTH
    q_ref[...] = (rope(q, cos64, sin64, HEAD_DIM // 2) * (HEAD_DIM ** -0.5)).astype(BF16)
    k = _dot(xb, wm_ref[:, o:o + ATTN_WIDTH]); o += ATTN_WIDTH
    k_ref[...] = rope(k, cos64, sin64, HEAD_DIM // 2).astype(BF16)
    va = _dot(xb, wm_ref[:, o:o + ATTN_HEADS * LANES]); o += ATTN_HEADS * LANES
    pad = lax.broadcasted_iota(jnp.int32, (1, ATTN_HEADS * LANES), 1) & (LANES - 1)
    va_ref[...] = jnp.where(pad >= HEAD_DIM, 1.0, va).astype(BF16)
    iq = _dot(xb, wm_ref[:, o:o + IDX_HEADS * IDX_DIM]); o += IDX_HEADS * IDX_DIM
    iq_ref[...] = rope(iq, cos32, sin32, IDX_DIM // 2).astype(BF16)
    z_ref[...] = _dot(xb, wm_ref[:, o:o + SSD_WIDTH]).astype(BF16); o += SSD_WIDTH
    xbc_ref[...] = _dot(xb, wm_ref[:, o:o + XBC_WIDTH]).astype(BF16)
    misc = _dot(xb, wx_ref[...])
    misc_ref[...] = jnp.where(lane < IDX_DIM, rope(misc, cos32, sin32, IDX_DIM // 2), misc)


def _inproj(x2, pos2, w_main, w_misc, inv64, inv32, tm):
    n, d = x2.shape
    wm = w_main.shape[1]
    row = lambda w: pl.BlockSpec((tm, w), lambda i: (i, 0))
    full = lambda a: pl.BlockSpec(a.shape, lambda i: (0, 0))
    outs = [(ATTN_WIDTH, BF16), (ATTN_WIDTH, BF16), (ATTN_HEADS * LANES, BF16),
            (IDX_HEADS * IDX_DIM, BF16), (SSD_WIDTH, BF16), (XBC_WIDTH, BF16), (LANES, F32)]
    return pl.pallas_call(
        _inproj_kernel,
        grid=(n // tm,),
        in_specs=[row(d), row(1), full(w_main), full(w_misc), full(inv64), full(inv32)],
        out_specs=[row(w) for w, _ in outs],
        out_shape=[jax.ShapeDtypeStruct((n, w), dt) for w, dt in outs],
        compiler_params=_cparams("parallel"),
        name="inproj_rope",
    )(x2, pos2, w_main, w_misc, inv64, inv32)


def _dsa_kernel(q_ref, k_ref, va_ref, iq_ref, mq_ref, mk_ref, o_ref,
                s_scr, m_scr, acc_scr, lg_scr, p_scr, a_scr, *, topk, tq):
    tk = tq
    j = pl.program_id(1)
    nch = j + 1
    kf = float(topk)

    iw_t = mq_ref[...].T[MISC_IW:MISC_IW + IDX_HEADS, :] * (1.0 / 16.0)
    iq = iq_ref[...]
    qpos = j * tq + lax.broadcasted_iota(jnp.int32, (tk, tq), 1)
    kio = lax.broadcasted_iota(jnp.int32, (tk, tq), 0)

    def score_chunk(c, carry):
        off = pl.multiple_of(c * tk, tk)
        ikc = mk_ref[pl.ds(off, tk), MISC_IK:MISC_IK + IDX_DIM].astype(BF16)
        for h in range(IDX_HEADS):
            lg_scr[h] = _dot_nt(ikc, iq[:, h * IDX_DIM:(h + 1) * IDX_DIM])
        sc = jnp.zeros((tk, tq), F32)
        for h in range(IDX_HEADS):
            sc = sc + iw_t[h:h + 1, :] * jnp.maximum(lg_scr[h], 0.0)
        s_scr[c] = jnp.where(c * tk + kio <= qpos, sc, -jnp.inf)
        return carry

    lax.fori_loop(0, nch, score_chunk, 0)

    def count(pred):
        def body(c, acc):
            s3 = s_scr[c].reshape(tk // 8, 8, tq)
            return acc + jnp.sum(jnp.where(pred(s3, c), 1.0, 0.0), axis=0)
        acc = lax.fori_loop(0, nch, body, jnp.zeros((8, tq), F32))
        return jnp.sum(acc, axis=0, keepdims=True)

    def rows8(t):
        return jnp.broadcast_to(t, (8, tq))[None]

    def count_ge(t):
        t8 = rows8(t)
        return count(lambda s3, c: s3 >= t8)

    pos_row = count_ge(jnp.zeros((1, tq), F32)) >= kf
    sgn = jnp.where(pos_row, 1.0, -1.0)

    def accept(trial_mag):
        ok = jnp.where(count_ge(sgn * trial_mag) >= kf, 1.0, -1.0)
        return ok == sgn

    mag = jnp.where(accept(jnp.full((1, tq), F32_MIN_NORMAL, F32)), F32_MIN_NORMAL, 0.0)
    for b in range(7, -1, -1):
        if b == 7:
            trial = (mag * float(2.0 ** 64)) * float(2.0 ** 64)
        else:
            trial = mag * float(2.0 ** (2 ** b))
        trial = jnp.minimum(trial, F32_MAX)
        mag = jnp.where(accept(trial), trial, mag)
    base = mag
    for kbit in range(1, 24):
        trial = jnp.minimum(mag + base * float(2.0 ** -kbit), F32_MAX)
        mag = jnp.where(accept(trial), trial, mag)
    ulp = jnp.where(base > 0.0, base * float(2.0 ** -23), F32_MIN_NORMAL)
    thr = jnp.where(pos_row, mag, -(mag + ulp))

    n_ge = count_ge(thr)

    @pl.when(jnp.max(n_ge) > kf)
    def _():
        t8 = rows8(thr)
        kio3 = kio.reshape(tk // 8, 8, tq)
        n_gt = count(lambda s3, c: s3 > t8)
        n_eq = n_ge - n_gt
        target = jnp.minimum(kf - n_gt, n_eq)
        mpos = jnp.zeros((1, tq), F32)
        nbits = int(math.ceil(math.log2(tk * s_scr.shape[0]))) + 1
        for b in range(nbits - 1, -1, -1):
            trial = mpos + float(2 ** b)
            tr8 = rows8(trial)
            cnt = count(lambda s3, c: jnp.where(s3 == t8, (c * tk + kio3).astype(F32), 1e9) < tr8)
            mpos = jnp.where(cnt < target, trial, mpos)

        def drop(c, carry):
            s = s_scr[c]
            kill = jnp.where(s == thr, (c * tk + kio).astype(F32), -1.0) > mpos
            s_scr[c] = jnp.where(kill, -jnp.inf, s)
            return carry

        lax.fori_loop(0, nch, drop, 0)

    def wide(t):
        return jnp.concatenate([t] * (tk // LANES), axis=1)

    thr_q = jnp.broadcast_to(jnp.maximum(thr, -F32_MAX), (LANES, tq)).T
    thr_w = wide(thr_q)
    m_scr[...] = jnp.full(m_scr.shape, NEG_BIG, F32)
    acc_scr[...] = jnp.zeros(acc_scr.shape, F32)

    def attn_chunk(c, carry):
        off = pl.multiple_of(c * tk, tk)
        sel = s_scr[c].T >= thr_w
        for h in range(ATTN_HEADS):
            hs = slice(h * HEAD_DIM, (h + 1) * HEAD_DIM)
            lg_scr[h] = _dot_nt(q_ref[:, hs], k_ref[pl.ds(off, tk), hs])
        for h in range(ATTN_HEADS):
            lg = jnp.where(sel, lg_scr[h], -jnp.inf)
            m_old = m_scr[h]
            m_new = jnp.maximum(m_old, jnp.max(lg, axis=1, keepdims=True))
            p_scr[h] = jnp.exp(lg - wide(m_new)).astype(BF16)
            a_scr[h] = jnp.exp(m_old - m_new)
            m_scr[h] = m_new
        for h in range(ATTN_HEADS):
            pv = _dot(p_scr[h], va_ref[pl.ds(off, tk), h * LANES:(h + 1) * LANES])
            acc_scr[h] = a_scr[h] * acc_scr[h] + pv
        return carry

    lax.fori_loop(0, nch, attn_chunk, 0)
    for h in range(ATTN_HEADS):
        a = acc_scr[h]
        o_ref[:, h * HEAD_DIM:(h + 1) * HEAD_DIM] = (a[:, :HEAD_DIM] / a[:, HEAD_DIM:]).astype(BF16)


def _dsa(q, k, va, iq, misc, bsz, seq, tq):
    n = bsz * seq
    nq = seq // tq
    topk = min(MAX_TOPK_KEYS, seq // 4)
    qblk = lambda w: pl.BlockSpec((tq, w), lambda b, j: (b * nq + j, 0))
    kblk = lambda w: pl.BlockSpec((seq, w), lambda b, j: (b, 0))
    return pl.pallas_call(
        functools.partial(_dsa_kernel, topk=topk, tq=tq),
        grid=(bsz, nq),
        in_specs=[qblk(ATTN_WIDTH), kblk(ATTN_WIDTH), kblk(ATTN_HEADS * LANES),
                  qblk(IDX_HEADS * IDX_DIM), qblk(LANES), kblk(LANES)],
        out_specs=qblk(ATTN_WIDTH),
        out_shape=jax.ShapeDtypeStruct((n, ATTN_WIDTH), BF16),
        scratch_shapes=[
            pltpu.VMEM((nq, tq, tq), F32),
            pltpu.VMEM((ATTN_HEADS, tq, LANES), F32),
            pltpu.VMEM((ATTN_HEADS, tq, LANES), F32),
            pltpu.VMEM((ATTN_HEADS, tq, tq), F32),
            pltpu.VMEM((ATTN_HEADS, tq, tq), BF16),
            pltpu.VMEM((ATTN_HEADS, tq, LANES), F32),
        ],
        compiler_params=_cparams("parallel", "arbitrary"),
        name="dsa_attention",
    )(q, k, va, iq, misc, misc)


def _ssd_kernel(z_ref, xbc_ref, misc_ref, cw_ref, cb_ref, dtb_ref, alog_ref, dsk_ref, nw_ref, e_ref,
                o_ref, tail_scr, st_scr):
    seq = z_ref.shape[0]
    t = CHUNK
    tail_scr[...] = jnp.zeros(tail_scr.shape, F32)
    st_scr[...] = jnp.zeros(st_scr.shape, F32)
    a_lane = -jnp.exp(alog_ref[...])
    r = lax.broadcasted_iota(jnp.int32, (t, t), 0)
    cidx = lax.broadcasted_iota(jnp.int32, (t, t), 1)
    lower = r >= cidx
    ltri = jnp.where(lower, 1.0, 0.0)
    expand = e_ref[...]
    gw = SSD_WIDTH // SSD_GROUPS
    hpg = SSD_HEADS // SSD_GROUPS

    def chunk(c, carry):
        off = pl.multiple_of(c * t, t)
        xin = xbc_ref[pl.ds(off, t), :].astype(F32)
        xp = jnp.concatenate([tail_scr[...], xin], axis=0)
        tail_scr[...] = xin[t - 8:, :]
        acc = jnp.broadcast_to(cb_ref[...], (t, XBC_WIDTH))
        for w in range(CONV_WIDTH):
            s0 = 8 - (CONV_WIDTH - 1) + w
            acc = acc + xp[s0:s0 + t, :] * cw_ref[w:w + 1, :]
        xc = _silu(acc)
        xs = xc[:, :SSD_WIDTH]
        bm = xc[:, SSD_WIDTH:SSD_WIDTH + SSD_GROUPS * D_STATE]
        cm = xc[:, SSD_WIDTH + SSD_GROUPS * D_STATE:]

        dt = _softplus(misc_ref[pl.ds(off, t), :] + dtb_ref[...])
        acum = _dot_exact(ltri, dt * a_lane)
        acum_t = acum.T
        dt_e = _dot_exact(dt, expand)
        ac_e = _dot_exact(acum, expand)
        alast_e = ac_e[t - 1:t, :]
        xdt = xs * dt_e
        xw = (xdt * jnp.exp(alast_e - ac_e)).astype(BF16)
        eac = jnp.exp(ac_e)
        xdt_b = xdt.astype(BF16)

        ys = []
        for g in range(SSD_GROUPS):
            gs = slice(g * gw, (g + 1) * gw)
            bg = bm[:, g * D_STATE:(g + 1) * D_STATE]
            cg = cm[:, g * D_STATE:(g + 1) * D_STATE].astype(BF16)
            cb = _dot_nt(cg, bg.astype(BF16))
            st_old = st_scr[:, gs]
            y_off = _dot(cg, st_old.astype(BF16)) * eac[:, gs]
            st_scr[:, gs] = jnp.exp(alast_e[:, gs]) * st_old + _dot(bg.T.astype(BF16), xw[:, gs])
            for hh in range(hpg):
                h = g * hpg + hh
                col = acum[:, MISC_DT + h:MISC_DT + h + 1]
                row = acum_t[MISC_DT + h:MISC_DT + h + 1, :]
                decay = jnp.exp(jnp.where(lower, col - row, -jnp.inf))
                wmat = (cb * decay).astype(BF16)
                hs = slice(h * SSD_HEAD_DIM, (h + 1) * SSD_HEAD_DIM)
                ys.append(_dot(wmat, xdt_b[:, hs]) + y_off[:, hh * SSD_HEAD_DIM:(hh + 1) * SSD_HEAD_DIM])
        y = jnp.concatenate(ys, axis=1) + xs * dsk_ref[...]
        y = y * _silu(z_ref[pl.ds(off, t), :].astype(F32))
        outs = []
        for g in range(SSD_GROUPS):
            yg = y[:, g * gw:(g + 1) * gw]
            outs.append(yg * lax.rsqrt(jnp.mean(yg * yg, axis=-1, keepdims=True) + LN_EPS))
        o_ref[pl.ds(off, t), :] = (jnp.concatenate(outs, axis=1) * nw_ref[...]).astype(BF16)
        return carry

    lax.fori_loop(0, seq // t, chunk, 0)


def _ssd(z, xbc, misc, conv_w, conv_b, dtb_lane, alog_lane, dsk, norm_w, expand, bsz, seq):
    n = bsz * seq
    blk = lambda w: pl.BlockSpec((seq, w), lambda b: (b, 0))
    full = lambda a: pl.BlockSpec(a.shape, lambda b: (0, 0))
    return pl.pallas_call(
        _ssd_kernel,
        grid=(bsz,),
        in_specs=[blk(SSD_WIDTH), blk(XBC_WIDTH), blk(LANES), full(conv_w), full(conv_b),
                  full(dtb_lane), full(alog_lane), full(dsk), full(norm_w), full(expand)],
        out_specs=blk(SSD_WIDTH),
        out_shape=jax.ShapeDtypeStruct((n, SSD_WIDTH), BF16),
        scratch_shapes=[pltpu.VMEM((8, XBC_WIDTH), F32),
                        pltpu.VMEM((D_STATE, SSD_WIDTH), F32)],
        compiler_params=_cparams("parallel"),
        name="ssd_mixer",
    )(z, xbc, misc, conv_w, conv_b, dtb_lane, alog_lane, dsk, norm_w, expand)


def _outproj_kernel(attn_ref, ssd_ref, x_ref, wa_ref, ws_ref, g_ref, b_ref, h_ref, hb_ref, hbt_ref, *, alpha):
    mix = _dot(attn_ref[...], wa_ref[...]) + _dot(ssd_ref[...], ws_ref[...])
    h = _layer_norm(alpha * x_ref[...] + mix, g_ref[...], b_ref[...])
    h_ref[...] = h
    hb_ref[...] = h.astype(BF16)
    hbt_ref[...] = h.T.astype(BF16)


def _outproj(attn, ssd, x2, wa, ws, g, b, alpha, tm):
    n, d = x2.shape
    row = lambda w: pl.BlockSpec((tm, w), lambda i: (i, 0))
    full = lambda a: pl.BlockSpec(a.shape, lambda i: (0, 0))
    return pl.pallas_call(
        functools.partial(_outproj_kernel, alpha=alpha),
        grid=(n // tm,),
        in_specs=[row(ATTN_WIDTH), row(SSD_WIDTH), row(d), full(wa), full(ws), full(g), full(b)],
        out_specs=[row(d), row(d), pl.BlockSpec((d, tm), lambda i: (0, i))],
        out_shape=[jax.ShapeDtypeStruct((n, d), F32), jax.ShapeDtypeStruct((n, d), BF16),
                   jax.ShapeDtypeStruct((d, n), BF16)],
        compiler_params=_cparams("parallel"),
        name="outproj_ln1",
    )(attn, ssd, x2, wa, ws, g, b)


def _peer_score_kernel(hb_ref, wq_ref, k1_ref, k2_ref, s1_ref, s2_ref):
    qb = _dot(hb_ref[...], wq_ref[...]).astype(BF16)
    k1 = k1_ref[...]
    k2 = k2_ref[...]
    for h in range(PEER_HEADS):
        o = h * 2 * PEER_HALF
        s1_ref[h] = _dot_nt(k1, qb[:, o:o + PEER_HALF])
        s2_ref[h] = _dot_nt(k2, qb[:, o + PEER_HALF:o + 2 * PEER_HALF])


def _peer_scores(hb, wq, k1, k2, tm):
    n, d = hb.shape
    full = lambda a: pl.BlockSpec(a.shape, lambda i: (0, 0))
    sblk = pl.BlockSpec((PEER_HEADS, N_KEYS, tm), lambda i: (0, 0, i))
    sshape = jax.ShapeDtypeStruct((PEER_HEADS, N_KEYS, n), F32)
    return pl.pallas_call(
        _peer_score_kernel,
        grid=(n // tm,),
        in_specs=[pl.BlockSpec((tm, d), lambda i: (i, 0)), full(wq), full(k1), full(k2)],
        out_specs=[sblk, sblk],
        out_shape=[sshape, sshape],
        compiler_params=_cparams("parallel"),
        name="peer_scores",
    )(hb, wq, k1, k2)


_CAND_COUNTS = [PEER_TOPK // (j1 + 1) for j1 in range(PEER_SUB_TOPK)]
_CAND_ROWS = sum(_CAND_COUNTS)
_CAND_PAD = -(-_CAND_ROWS // 8) * 8


def _bitonic_desc(xs, full):
    xs = list(xs)
    n = len(xs)
    k = 2 if full else n
    while k <= n:
        j = k // 2
        while j >= 1:
            for i in range(n):
                l = i ^ j
                if l > i:
                    hi, lo = jnp.maximum(xs[i], xs[l]), jnp.minimum(xs[i], xs[l])
                    xs[i], xs[l] = (hi, lo) if (i & k) == 0 else (lo, hi)
            j //= 2
        k *= 2
    return xs


def _peer_select_kernel(s1_ref, s2_ref, pos_ref, ea_ref, n1_ref, eb_ref, r2_ref,
                        v_scr, rank_scr, cand_scr, pick_scr):
    tl = s1_ref.shape[2]
    kio = lax.broadcasted_iota(jnp.int32, (N_KEYS, tl), 0).astype(F32)
    cpos = jnp.broadcast_to(pos_ref[...], (_CAND_PAD, tl))
    not_ranked = float(PEER_SUB_TOPK)

    def top_ranks(s, slot, exact):
        rank = jnp.full((N_KEYS, tl), not_ranked, F32)
        for jj in range(PEER_SUB_TOPK):
            m = jnp.max(s, axis=0, keepdims=True)
            hit = s == m
            if exact:
                hit = kio == jnp.min(jnp.where(hit, kio, float(N_KEYS)), axis=0, keepdims=True)
            s = jnp.where(hit, -jnp.inf, s)
            rank = jnp.where(hit, float(jj), rank)
            v_scr[slot, jj:jj + 1, :] = m
        rank_scr[slot] = rank
        return jnp.max(jnp.sum(jnp.where(rank < not_ranked, 1.0, 0.0), axis=0, keepdims=True))

    def top_sorted(s, slot):
        nv = N_KEYS // 8
        xs = _bitonic_desc([s[8 * i:8 * (i + 1), :] for i in range(nv)], full=True)
        for shift in (4, 2, 1):
            ys = [pltpu.roll(x, shift, 0) for x in xs]
            xs = _bitonic_desc([jnp.maximum(xs[i], ys[nv - 1 - i]) for i in range(nv)], full=False)
        xs = xs[:PEER_SUB_TOPK]
        rank = jnp.concatenate(
            [sum(jnp.where(v > s[8 * i:8 * (i + 1), :], 1.0, 0.0) for v in xs) for i in range(nv)], axis=0)
        for jj in range(PEER_SUB_TOPK):
            v_scr[slot, jj:jj + 1, :] = xs[jj][0:1, :]
        rank_scr[slot] = rank
        marked = jnp.max(jnp.sum(jnp.where(rank < not_ranked, 1.0, 0.0), axis=0, keepdims=True))
        equal = sum(jnp.where(xs[jj] == xs[jj + 1], 1.0, 0.0) for jj in range(PEER_SUB_TOPK - 1))
        return marked + jnp.max(equal)

    def pick_pairs(exact):
        cand = cand_scr[...]
        picked = jnp.zeros((_CAND_PAD, tl), F32)
        for kk in range(PEER_TOPK):
            m = jnp.max(cand, axis=0, keepdims=True)
            hit = cand == m
            if exact:
                hit = cpos == jnp.min(jnp.where(hit, cpos, 1e9), axis=0, keepdims=True)
            cand = jnp.where(hit, -jnp.inf, cand)
            picked = jnp.where(hit, 1.0, picked)
        pick_scr[...] = picked
        return jnp.max(jnp.sum(picked, axis=0, keepdims=True))

    def head(h, carry):
        s1 = s1_ref[h]
        s2 = s2_ref[h]
        marked = jnp.maximum(top_sorted(s1, 0), top_sorted(s2, 1))

        @pl.when(marked > float(PEER_SUB_TOPK))
        def _():
            top_ranks(s1, 0, True)
            top_ranks(s2, 1, True)

        v1 = v_scr[0]
        v2 = v_scr[1]
        o = 0
        for j1, cnt in enumerate(_CAND_COUNTS):
            cand_scr[o:o + cnt, :] = v1[j1:j1 + 1, :] + v2[0:cnt, :]
            o += cnt
        if _CAND_PAD > _CAND_ROWS:
            cand_scr[_CAND_ROWS:, :] = jnp.full((_CAND_PAD - _CAND_ROWS, tl), -jnp.inf, F32)
        npicked = pick_pairs(False)

        @pl.when(npicked > float(PEER_TOPK))
        def _():
            pick_pairs(True)

        picked = pick_scr[...]
        best0 = v1[0:1, :] + v2[0:1, :]
        zsum = jnp.sum(jnp.where(picked > 0.0, jnp.exp(cand_scr[...] - best0), 0.0), axis=0, keepdims=True)
        rank1 = rank_scr[0]
        n1 = jnp.zeros((N_KEYS, tl), F32)
        o = 0
        for j1, cnt in enumerate(_CAND_COUNTS):
            nsel = jnp.sum(pick_scr[o:o + cnt, :], axis=0, keepdims=True)
            n1 = jnp.where(rank1 == float(j1), nsel, n1)
            o += cnt
        ea_ref[h] = jnp.exp(s1 - v1[0:1, :]) * (0.5 / zsum)
        n1_ref[h] = n1
        eb_ref[h] = jnp.exp(s2 - v2[0:1, :]).astype(BF16)
        r2_ref[h] = rank_scr[1].astype(BF16)
        return carry

    lax.fori_loop(0, PEER_HEADS, head, 0)


def _peer_select(s1t, s2t, cand_pos, tl):
    n = s1t.shape[2]
    blk = pl.BlockSpec((PEER_HEADS, N_KEYS, tl), lambda i: (0, 0, i))
    shp = lambda dt: jax.ShapeDtypeStruct((PEER_HEADS, N_KEYS, n), dt)
    return pl.pallas_call(
        _peer_select_kernel,
        grid=(n // tl,),
        in_specs=[blk, blk, pl.BlockSpec(cand_pos.shape, lambda i: (0, 0))],
        out_specs=[blk, blk, blk, blk],
        out_shape=[shp(F32), shp(F32), shp(BF16), shp(BF16)],
        scratch_shapes=[pltpu.VMEM((2, PEER_SUB_TOPK, tl), F32),
                        pltpu.VMEM((2, N_KEYS, tl), F32),
                        pltpu.VMEM((_CAND_PAD, tl), F32),
                        pltpu.VMEM((_CAND_PAD, tl), F32)],
        compiler_params=_cparams("parallel"),
        name="peer_select",
    )(s1t, s2t, cand_pos)


PEER_TOK = 256
BF16_ROWS = 16


def _peer_expert_kernel(ht_ref, u_ref, vt_ref, ea_ref, n1_ref, eb_ref, r2_ref, o_ref,
                        act0_scr, act1_scr, acc_scr, pt0_scr, pt1_scr):
    e = pl.program_id(1)
    es = u_ref.shape[0]
    tl = ht_ref.shape[1]
    grp = N_KEYS // BF16_ROWS

    @pl.when(e == 0)
    def _():
        acc_scr[...] = jnp.zeros(acc_scr.shape, F32)
        act1_scr[...] = jnp.zeros(act1_scr.shape, BF16)

    def step(act_w, act_r):
        zero = jnp.zeros((), BF16)
        npc = tl // PEER_TOK

        pt_bufs = (pt0_scr, pt1_scr)

        def first_matmul(p):
            pt_bufs[p % 2][...] = _dot(u_ref[...], ht_ref[:, p * PEER_TOK:(p + 1) * PEER_TOK])

        def second_matmul(p):
            cols = slice(p * PEER_TOK, (p + 1) * PEER_TOK)
            acc_scr[:, cols] += _dot(vt_ref[...], act_r[:, cols])

        def gate_block(p):
            cols = slice(p * PEER_TOK, (p + 1) * PEER_TOK)
            pt = pt_bufs[p % 2]

            def rows16(ref, h, cc):
                return jnp.broadcast_to(ref[h, cc:cc + 1, cols], (BF16_ROWS, PEER_TOK)).astype(BF16)[None]

            for cc in range(es // N_KEYS):
                g = jnp.zeros((grp, BF16_ROWS, PEER_TOK), BF16)
                for h in range(PEER_HEADS):
                    r2 = r2_ref[h, :, cols].reshape(grp, BF16_ROWS, PEER_TOK)
                    eb = eb_ref[h, :, cols].reshape(grp, BF16_ROWS, PEER_TOK)
                    g = g + jnp.where(r2 < rows16(n1_ref, h, cc), eb, zero) * rows16(ea_ref, h, cc)
                x = pt[cc * N_KEYS:(cc + 1) * N_KEYS, :]
                gelu2 = x * (1.0 + lax.erf(x * (2.0 ** -0.5)))
                act_w[cc * N_KEYS:(cc + 1) * N_KEYS, cols] = gelu2.astype(BF16) * g.reshape(N_KEYS, PEER_TOK)


        first_matmul(0)
        for p in range(npc):
            if p + 1 < npc:
                first_matmul(p + 1)
            second_matmul(p)
            gate_block(p)

    @pl.when(lax.rem(e, 2) == 0)
    def _():
        step(act0_scr, act1_scr)

    @pl.when(lax.rem(e, 2) == 1)
    def _():
        step(act1_scr, act0_scr)

    @pl.when(e == pl.num_programs(1) - 1)
    def _():
        o_ref[...] = acc_scr[...].T


def _peer_experts(ht, u, vt, ea, n1, eb, r2, tl, es):
    d, n = ht.shape
    ng = u.shape[0] // es
    rows = es // N_KEYS
    cur_g = lambda e: jnp.minimum(e, ng - 1)
    a_blk = pl.BlockSpec((PEER_HEADS, rows, tl), lambda i, e: (0, cur_g(e), i))
    b_blk = pl.BlockSpec((PEER_HEADS, N_KEYS, tl), lambda i, e: (0, 0, i))
    return pl.pallas_call(
        _peer_expert_kernel,
        grid=(n // tl, ng + 1),
        in_specs=[pl.BlockSpec((d, tl), lambda i, e: (0, i)),
                  pl.BlockSpec((es, d), lambda i, e: (cur_g(e), 0)),
                  pl.BlockSpec((d, es), lambda i, e: (0, jnp.maximum(e - 1, 0))),
                  a_blk, a_blk, b_blk, b_blk],
        out_specs=pl.BlockSpec((tl, d), lambda i, e: (i, 0)),
        out_shape=jax.ShapeDtypeStruct((n, d), F32),
        scratch_shapes=[pltpu.VMEM((es, tl), BF16), pltpu.VMEM((es, tl), BF16), pltpu.VMEM((d, tl), F32),
                        pltpu.VMEM((es, PEER_TOK), F32), pltpu.VMEM((es, PEER_TOK), F32)],
        compiler_params=_cparams("parallel", "arbitrary"),
        name="peer_experts",
    )(ht, u, vt, ea, n1, eb, r2)


def _ln2_kernel(h_ref, f_ref, g_ref, b_ref, o_ref, *, alpha):
    o_ref[...] = _layer_norm(alpha * h_ref[...] + f_ref[...], g_ref[...], b_ref[...])


def _ln2(h, ffn, g, b, alpha, tm):
    n, d = h.shape
    row = pl.BlockSpec((tm, d), lambda i: (i, 0))
    full = lambda a: pl.BlockSpec(a.shape, lambda i: (0, 0))
    return pl.pallas_call(
        functools.partial(_ln2_kernel, alpha=alpha),
        grid=(n // tm,),
        in_specs=[row, row, full(g), full(b)],
        out_specs=row,
        out_shape=jax.ShapeDtypeStruct((n, d), F32),
        compiler_params=_cparams("parallel"),
        name="residual_ln2",
    )(h, ffn, g, b)


def _pick(n, pref):
    t = min(pref, n)
    while n % t:
        t //= 2
    return t


def _layer(x, positions, w_in, conv_w, conv_b, dt_bias, a_log, d_skip, ssm_norm_w, w_out,
           ln1_g, ln1_b, peer_wq, peer_k1, peer_k2, peer_u, peer_v, ln2_g, ln2_b, alpha):
    bsz, seq, d = x.shape
    n = bsz * seq
    x2 = x.reshape(n, d)
    pos2 = positions.reshape(n, 1)

    sizes = (ATTN_WIDTH, ATTN_WIDTH, ATTN_WIDTH, IDX_HEADS * IDX_DIM, IDX_DIM, IDX_HEADS,
             SSD_WIDTH, XBC_WIDTH, SSD_HEADS)
    offs = [0]
    for s in sizes:
        offs.append(offs[-1] + s)
    col = lambda i: w_in[:, offs[i]:offs[i + 1]]
    wv = jnp.pad(col(2).reshape(d, ATTN_HEADS, HEAD_DIM), ((0, 0), (0, 0), (0, LANES - HEAD_DIM)))
    w_main = jnp.concatenate([col(0), col(1), wv.reshape(d, ATTN_HEADS * LANES), col(3), col(6), col(7)],
                             axis=1).astype(BF16)
    w_misc = jnp.concatenate(
        [col(4), col(5), col(8), jnp.zeros((d, LANES - IDX_DIM - IDX_HEADS - SSD_HEADS), w_in.dtype)],
        axis=1).astype(BF16)
    lane = jnp.arange(LANES)
    inv64 = (ROPE_THETA ** (-jnp.arange(0, HEAD_DIM, 2, dtype=F32) / HEAD_DIM))[(lane % HEAD_DIM) % (HEAD_DIM // 2)]
    inv32 = (ROPE_THETA ** (-jnp.arange(0, IDX_DIM, 2, dtype=F32) / IDX_DIM))[(lane % IDX_DIM) % (IDX_DIM // 2)]

    q, k, va, iq, z, xbc, misc = _inproj(x2, pos2, w_main, w_misc, inv64[None, :], inv32[None, :], _pick(n, 512))

    attn = _dsa(q, k, va, iq, misc, bsz, seq, _pick(seq, 256))

    pad_dt = lambda a: jnp.zeros((1, LANES), F32).at[0, MISC_DT:MISC_DT + SSD_HEADS].set(a.astype(F32))
    expand = (jnp.arange(LANES)[:, None] == MISC_DT + jnp.arange(SSD_WIDTH)[None, :] // SSD_HEAD_DIM).astype(F32)
    ssd = _ssd(z, xbc, misc, conv_w.astype(F32), conv_b.astype(F32)[None, :], pad_dt(dt_bias), pad_dt(a_log),
               jnp.repeat(d_skip.astype(F32), SSD_HEAD_DIM)[None, :], ssm_norm_w.astype(F32)[None, :],
               expand, bsz, seq)

    wob = w_out.astype(BF16)
    h, hb, hbt = _outproj(attn, ssd, x2, wob[:ATTN_WIDTH], wob[ATTN_WIDTH:], ln1_g[None, :], ln1_b[None, :],
                          alpha, _pick(n, 512))

    s1t, s2t = _peer_scores(hb, peer_wq.astype(BF16), peer_k1.astype(BF16), peer_k2.astype(BF16), _pick(n, 512))

    cand_pos = []
    for j1, cnt in enumerate(_CAND_COUNTS):
        cand_pos += [float(j1 * PEER_SUB_TOPK + j2) for j2 in range(cnt)]
    cand_pos += [1e9] * (_CAND_PAD - _CAND_ROWS)
    ea, n1, eb, r2 = _peer_select(s1t, s2t, jnp.asarray(cand_pos, F32)[:, None], _pick(n, 512))

    ffn = _peer_experts(hbt, peer_u.astype(BF16), peer_v.T.astype(BF16), ea, n1, eb, r2,
                        _pick(n, 1024), 8 * N_KEYS)

    out = _ln2(h, ffn, ln2_g[None, :], ln2_b[None, :], alpha, _pick(n, 512))
    return out.reshape(bsz, seq, d)


def kernel(x, positions, w_in, conv_w, conv_b, dt_bias, a_log, d_skip, ssm_norm_w, w_out, ln1_g, ln1_b,
           peer_wq, peer_k1, peer_k2, peer_u, peer_v, ln2_g, ln2_b):
    depth = w_in.shape[0]
    alpha = float((2 * depth) ** 0.25)
    for i in range(depth):
        x = _layer(x, positions, w_in[i], conv_w[i], conv_b[i], dt_bias[i], a_log[i], d_skip[i],
                   ssm_norm_w[i], w_out[i], ln1_g[i], ln1_b[i], peer_wq[i], peer_k1[i], peer_k2[i],
                   peer_u[i], peer_v[i], ln2_g[i], ln2_b[i], alpha)
    return x
```

```python
import functools
import math

import jax
import jax.numpy as jnp
from jax import lax
from jax.experimental import pallas as pl
from jax.experimental.pallas import tpu as pltpu

F32 = jnp.float32
BF16 = jnp.bfloat16

ATTN_HEADS = 8
HEAD_DIM = 64
ATTN_WIDTH = ATTN_HEADS * HEAD_DIM
IDX_HEADS = 8
IDX_DIM = 32
MAX_TOPK_KEYS = 256
SSD_HEADS = 8
SSD_HEAD_DIM = 64
SSD_WIDTH = SSD_HEADS * SSD_HEAD_DIM
SSD_GROUPS = 2
D_STATE = 128
CONV_WIDTH = 4
CHUNK = 128
XBC_WIDTH = SSD_WIDTH + 2 * SSD_GROUPS * D_STATE
PEER_HEADS = 8
N_KEYS = 128
PEER_HALF = 128
PEER_SUB_TOPK = 16
PEER_TOPK = 16
ROPE_THETA = 10000.0
LN_EPS = 1e-5

MISC_IK = 0
MISC_IW = IDX_DIM
MISC_DT = IDX_DIM + IDX_HEADS
LANES = 128

F32_MIN_NORMAL = float(2.0 ** -126)
F32_MAX = float((2.0 - 2.0 ** -23) * 2.0 ** 127)
NEG_BIG = -1e30
ATTN_ROWS = 128

VMEM_LIMIT = 56 * 1024 * 1024


def _cparams(*sem):
    return pltpu.CompilerParams(dimension_semantics=sem, vmem_limit_bytes=VMEM_LIMIT)


def _dot(a, b):
    return jnp.dot(a, b, preferred_element_type=F32)


def _dot_nt(a, b):
    return lax.dot_general(a, b, (((1,), (1,)), ((), ())), preferred_element_type=F32)


def _dot_exact(a, b):
    return jnp.dot(a, b, preferred_element_type=F32, precision=lax.Precision.HIGHEST)


def _silu(x):
    return x / (1.0 + jnp.exp(-x))


def _softplus(x):
    return jnp.maximum(x, 0.0) + jnp.log(1.0 + jnp.exp(-jnp.abs(x)))


def _layer_norm(y, g, b):
    mu = jnp.mean(y, axis=-1, keepdims=True)
    d = y - mu
    var = jnp.mean(d * d, axis=-1, keepdims=True)
    return d * lax.rsqrt(var + LN_EPS) * g + b


def _swap_halves(x, half):
    w = x.shape[-1]
    lane = lax.broadcasted_iota(jnp.int32, x.shape, x.ndim - 1)
    first = (lane & (2 * half - 1)) < half
    return jnp.where(first, pltpu.roll(x, w - half, x.ndim - 1), pltpu.roll(x, half, x.ndim - 1))


def _inproj_kernel(x_ref, pos_ref, wm_ref, wx_ref, inv64_ref, inv32_ref,
                   q_ref, k_ref, va_ref, iq_ref, z_ref, xbc_ref, misc_ref):
    xb = x_ref[...].astype(BF16)
    pos = pos_ref[...].astype(F32)
    lane = lax.broadcasted_iota(jnp.int32, (1, LANES), 1)

    ang64 = pos * inv64_ref[...]
    cos64 = jnp.cos(ang64)
    sin64 = jnp.where((lane & (HEAD_DIM - 1)) < HEAD_DIM // 2, -1.0, 1.0) * jnp.sin(ang64)
    ang32 = pos * inv32_ref[...]
    cos32 = jnp.cos(ang32)
    sin32 = jnp.where((lane & (IDX_DIM - 1)) < IDX_DIM // 2, -1.0, 1.0) * jnp.sin(ang32)

    def rope(t, cos_t, sin_t, half):
        reps = t.shape[-1] // LANES
        c = jnp.concatenate([cos_t] * reps, axis=1) if reps > 1 else cos_t
        s = jnp.concatenate([sin_t] * reps, axis=1) if reps > 1 else sin_t
        return t * c + _swap_halves(t, half) * s

    o = 0
    q = _dot(xb, wm_ref[:, o:o + ATTN_WIDTH]); o += ATTN_WIDTH
    q_ref[...] = (rope(q, cos64, sin64, HEAD_DIM // 2) * (HEAD_DIM ** -0.5)).astype(BF16)
    k = _dot(xb, wm_ref[:, o:o + ATTN_WIDTH]); o += ATTN_WIDTH
    k_ref[...] = rope(k, cos64, sin64, HEAD_DIM // 2).astype(BF16)
    va = _dot(xb, wm_ref[:, o:o + ATTN_HEADS * LANES]); o += ATTN_HEADS * LANES
    pad = lax.broadcasted_iota(jnp.int32, (1, ATTN_HEADS * LANES), 1) & (LANES - 1)
    va_ref[...] = jnp.where(pad >= HEAD_DIM, 1.0, va).astype(BF16)
    iq = _dot(xb, wm_ref[:, o:o + IDX_HEADS * IDX_DIM]); o += IDX_HEADS * IDX_DIM
    iq_ref[...] = rope(iq, cos32, sin32, IDX_DIM // 2).astype(BF16)
    z_ref[...] = _dot(xb, wm_ref[:, o:o + SSD_WIDTH]).astype(BF16); o += SSD_WIDTH
    xbc_ref[...] = _dot(xb, wm_ref[:, o:o + XBC_WIDTH]).astype(BF16)
    misc = _dot(xb, wx_ref[...])
    misc_ref[...] = jnp.where(lane < IDX_DIM, rope(misc, cos32, sin32, IDX_DIM // 2), misc)


def _inproj(x2, pos2, w_main, w_misc, inv64, inv32, tm):
    n, d = x2.shape
    wm = w_main.shape[1]
    row = lambda w: pl.BlockSpec((tm, w), lambda i: (i, 0))
    full = lambda a: pl.BlockSpec(a.shape, lambda i: (0, 0))
    outs = [(ATTN_WIDTH, BF16), (ATTN_WIDTH, BF16), (ATTN_HEADS * LANES, BF16),
            (IDX_HEADS * IDX_DIM, BF16), (SSD_WIDTH, BF16), (XBC_WIDTH, BF16), (LANES, F32)]
    return pl.pallas_call(
        _inproj_kernel,
        grid=(n // tm,),
        in_specs=[row(d), row(1), full(w_main), full(w_misc), full(inv64), full(inv32)],
        out_specs=[row(w) for w, _ in outs],
        out_shape=[jax.ShapeDtypeStruct((n, w), dt) for w, dt in outs],
        compiler_params=_cparams("parallel"),
        name="inproj_rope",
    )(x2, pos2, w_main, w_misc, inv64, inv32)


def _dsa_kernel(q_ref, k_ref, va_ref, iq_ref, mq_ref, mk_ref, o_ref,
                s_scr, m_scr, acc_scr, lg_scr, p_scr, a_scr, *, topk, tq):
    tk = tq
    j = pl.program_id(1)
    nch = j + 1
    kf = float(topk)

    iw_t = mq_ref[...].T[MISC_IW:MISC_IW + IDX_HEADS, :] * (1.0 / 16.0)
    iq = iq_ref[...]
    qpos = j * tq + lax.broadcasted_iota(jnp.int32, (tk, tq), 1)
    kio = lax.broadcasted_iota(jnp.int32, (tk, tq), 0)

    def score_chunk(c, carry):
        off = pl.multiple_of(c * tk, tk)
        ikc = mk_ref[pl.ds(off, tk), MISC_IK:MISC_IK + IDX_DIM].astype(BF16)
        for h in range(IDX_HEADS):
            lg_scr[h] = _dot_nt(ikc, iq[:, h * IDX_DIM:(h + 1) * IDX_DIM])
        sc = jnp.zeros((tk, tq), F32)
        for h in range(IDX_HEADS):
            sc = sc + iw_t[h:h + 1, :] * jnp.maximum(lg_scr[h], 0.0)
        s_scr[c] = jnp.where(c * tk + kio <= qpos, sc, -jnp.inf)
        return carry

    lax.fori_loop(0, nch, score_chunk, 0)

    def count(pred):
        def body(c, acc):
            s3 = s_scr[c].reshape(tk // 8, 8, tq)
            return acc + jnp.sum(jnp.where(pred(s3, c), 1.0, 0.0), axis=0)
        acc = lax.fori_loop(0, nch, body, jnp.zeros((8, tq), F32))
        return jnp.sum(acc, axis=0, keepdims=True)

    def rows8(t):
        return jnp.broadcast_to(t, (8, tq))[None]

    def count_ge(t):
        t8 = rows8(t)
        return count(lambda s3, c: s3 >= t8)

    pos_row = count_ge(jnp.zeros((1, tq), F32)) >= kf
    sgn = jnp.where(pos_row, 1.0, -1.0)

    def accept(trial_mag):
        ok = jnp.where(count_ge(sgn * trial_mag) >= kf, 1.0, -1.0)
        return ok == sgn

    mag = jnp.where(accept(jnp.full((1, tq), F32_MIN_NORMAL, F32)), F32_MIN_NORMAL, 0.0)
    for b in range(7, -1, -1):
        if b == 7:
            trial = (mag * float(2.0 ** 64)) * float(2.0 ** 64)
        else:
            trial = mag * float(2.0 ** (2 ** b))
        trial = jnp.minimum(trial, F32_MAX)
        mag = jnp.where(accept(trial), trial, mag)
    base = mag
    for kbit in range(1, 24):
        trial = jnp.minimum(mag + base * float(2.0 ** -kbit), F32_MAX)
        mag = jnp.where(accept(trial), trial, mag)
    ulp = jnp.where(base > 0.0, base * float(2.0 ** -23), F32_MIN_NORMAL)
    thr = jnp.where(pos_row, mag, -(mag + ulp))

    n_ge = count_ge(thr)

    @pl.when(jnp.max(n_ge) > kf)
    def _():
        t8 = rows8(thr)
        kio3 = kio.reshape(tk // 8, 8, tq)
        n_gt = count(lambda s3, c: s3 > t8)
        n_eq = n_ge - n_gt
        target = jnp.minimum(kf - n_gt, n_eq)
        mpos = jnp.zeros((1, tq), F32)
        nbits = int(math.ceil(math.log2(tk * s_scr.shape[0]))) + 1
        for b in range(nbits - 1, -1, -1):
            trial = mpos + float(2 ** b)
            tr8 = rows8(trial)
            cnt = count(lambda s3, c: jnp.where(s3 == t8, (c * tk + kio3).astype(F32), 1e9) < tr8)
            mpos = jnp.where(cnt < target, trial, mpos)

        def drop(c, carry):
            s = s_scr[c]
            kill = jnp.where(s == thr, (c * tk + kio).astype(F32), -1.0) > mpos
            s_scr[c] = jnp.where(kill, -jnp.inf, s)
            return carry

        lax.fori_loop(0, nch, drop, 0)

    def wide(t):
        return jnp.concatenate([t] * (tk // LANES), axis=1)

    thr_q = jnp.broadcast_to(jnp.maximum(thr, -F32_MAX), (LANES, tq)).T
    thr_w = wide(thr_q)
    m_scr[...] = jnp.full(m_scr.shape, NEG_BIG, F32)
    acc_scr[...] = jnp.zeros(acc_scr.shape, F32)

    def attn_chunk(c, carry):
        off = pl.multiple_of(c * tk, tk)
        sel = s_scr[c].T >= thr_w
        for h in range(ATTN_HEADS):
            hs = slice(h * HEAD_DIM, (h + 1) * HEAD_DIM)
            lg_scr[h] = _dot_nt(q_ref[:, hs], k_ref[pl.ds(off, tk), hs])
        for h in range(ATTN_HEADS):
            lg = jnp.where(sel, lg_scr[h], -jnp.inf)
            m_old = m_scr[h]
            m_new = jnp.maximum(m_old, jnp.max(lg, axis=1, keepdims=True))
            p_scr[h] = jnp.exp(lg - wide(m_new)).astype(BF16)
            a_scr[h] = jnp.exp(m_old - m_new)
            m_scr[h] = m_new
        for h in range(ATTN_HEADS):
            pv = _dot(p_scr[h], va_ref[pl.ds(off, tk), h * LANES:(h + 1) * LANES])
            acc_scr[h] = a_scr[h] * acc_scr[h] + pv
        return carry

    lax.fori_loop(0, nch, attn_chunk, 0)
    for h in range(ATTN_HEADS):
        a = acc_scr[h]
        o_ref[:, h * HEAD_DIM:(h + 1) * HEAD_DIM] = (a[:, :HEAD_DIM] / a[:, HEAD_DIM:]).astype(BF16)


def _dsa(q, k, va, iq, misc, bsz, seq, tq):
    n = bsz * seq
    nq = seq // tq
    topk = min(MAX_TOPK_KEYS, seq // 4)
    qblk = lambda w: pl.BlockSpec((tq, w), lambda b, j: (b * nq + j, 0))
    kblk = lambda w: pl.BlockSpec((seq, w), lambda b, j: (b, 0))
    return pl.pallas_call(
        functools.partial(_dsa_kernel, topk=topk, tq=tq),
        grid=(bsz, nq),
        in_specs=[qblk(ATTN_WIDTH), kblk(ATTN_WIDTH), kblk(ATTN_HEADS * LANES),
                  qblk(IDX_HEADS * IDX_DIM), qblk(LANES), kblk(LANES)],
        out_specs=qblk(ATTN_WIDTH),
        out_shape=jax.ShapeDtypeStruct((n, ATTN_WIDTH), BF16),
        scratch_shapes=[
            pltpu.VMEM((nq, tq, tq), F32),
            pltpu.VMEM((ATTN_HEADS, tq, LANES), F32),
            pltpu.VMEM((ATTN_HEADS, tq, LANES), F32),
            pltpu.VMEM((ATTN_HEADS, tq, tq), F32),
            pltpu.VMEM((ATTN_HEADS, tq, tq), BF16),
            pltpu.VMEM((ATTN_HEADS, tq, LANES), F32),
        ],
        compiler_params=_cparams("parallel", "arbitrary"),
        name="dsa_attention",
    )(q, k, va, iq, misc, misc)


def _ssd_kernel(z_ref, xbc_ref, misc_ref, cw_ref, cb_ref, dtb_ref, alog_ref, dsk_ref, nw_ref, e_ref,
                o_ref, tail_scr, st_scr):
    seq = z_ref.shape[0]
    t = CHUNK
    tail_scr[...] = jnp.zeros(tail_scr.shape, F32)
    st_scr[...] = jnp.zeros(st_scr.shape, F32)
    a_lane = -jnp.exp(alog_ref[...])
    r = lax.broadcasted_iota(jnp.int32, (t, t), 0)
    cidx = lax.broadcasted_iota(jnp.int32, (t, t), 1)
    lower = r >= cidx
    ltri = jnp.where(lower, 1.0, 0.0)
    expand = e_ref[...]
    gw = SSD_WIDTH // SSD_GROUPS
    hpg = SSD_HEADS // SSD_GROUPS

    def chunk(c, carry):
        off = pl.multiple_of(c * t, t)
        xin = xbc_ref[pl.ds(off, t), :].astype(F32)
        xp = jnp.concatenate([tail_scr[...], xin], axis=0)
        tail_scr[...] = xin[t - 8:, :]
        acc = jnp.broadcast_to(cb_ref[...], (t, XBC_WIDTH))
        for w in range(CONV_WIDTH):
            s0 = 8 - (CONV_WIDTH - 1) + w
            acc = acc + xp[s0:s0 + t, :] * cw_ref[w:w + 1, :]
        xc = _silu(acc)
        xs = xc[:, :SSD_WIDTH]
        bm = xc[:, SSD_WIDTH:SSD_WIDTH + SSD_GROUPS * D_STATE]
        cm = xc[:, SSD_WIDTH + SSD_GROUPS * D_STATE:]

        dt = _softplus(misc_ref[pl.ds(off, t), :] + dtb_ref[...])
        acum = _dot_exact(ltri, dt * a_lane)
        acum_t = acum.T
        dt_e = _dot_exact(dt, expand)
        ac_e = _dot_exact(acum, expand)
        alast_e = ac_e[t - 1:t, :]
        xdt = xs * dt_e
        xw = (xdt * jnp.exp(alast_e - ac_e)).astype(BF16)
        eac = jnp.exp(ac_e)
        xdt_b = xdt.astype(BF16)

        ys = []
        for g in range(SSD_GROUPS):
            gs = slice(g * gw, (g + 1) * gw)
            bg = bm[:, g * D_STATE:(g + 1) * D_STATE]
            cg = cm[:, g * D_STATE:(g + 1) * D_STATE].astype(BF16)
            cb = _dot_nt(cg, bg.astype(BF16))
            st_old = st_scr[:, gs]
            y_off = _dot(cg, st_old.astype(BF16)) * eac[:, gs]
            st_scr[:, gs] = jnp.exp(alast_e[:, gs]) * st_old + _dot(bg.T.astype(BF16), xw[:, gs])
            for hh in range(hpg):
                h = g * hpg + hh
                col = acum[:, MISC_DT + h:MISC_DT + h + 1]
                row = acum_t[MISC_DT + h:MISC_DT + h + 1, :]
                decay = jnp.exp(jnp.where(lower, col - row, -jnp.inf))
                wmat = (cb * decay).astype(BF16)
                hs = slice(h * SSD_HEAD_DIM, (h + 1) * SSD_HEAD_DIM)
                ys.append(_dot(wmat, xdt_b[:, hs]) + y_off[:, hh * SSD_HEAD_DIM:(hh + 1) * SSD_HEAD_DIM])
        y = jnp.concatenate(ys, axis=1) + xs * dsk_ref[...]
        y = y * _silu(z_ref[pl.ds(off, t), :].astype(F32))
        outs = []
        for g in range(SSD_GROUPS):
            yg = y[:, g * gw:(g + 1) * gw]
            outs.append(yg * lax.rsqrt(jnp.mean(yg * yg, axis=-1, keepdims=True) + LN_EPS))
        o_ref[pl.ds(off, t), :] = (jnp.concatenate(outs, axis=1) * nw_ref[...]).astype(BF16)
        return carry

    lax.fori_loop(0, seq // t, chunk, 0)


def _ssd(z, xbc, misc, conv_w, conv_b, dtb_lane, alog_lane, dsk, norm_w, expand, bsz, seq):
    n = bsz * seq
    blk = lambda w: pl.BlockSpec((seq, w), lambda b: (b, 0))
    full = lambda a: pl.BlockSpec(a.shape, lambda b: (0, 0))
    return pl.pallas_call(
        _ssd_kernel,
        grid=(bsz,),
        in_specs=[blk(SSD_WIDTH), blk(XBC_WIDTH), blk(LANES), full(conv_w), full(conv_b),
                  full(dtb_lane), full(alog_lane), full(dsk), full(norm_w), full(expand)],
        out_specs=blk(SSD_WIDTH),
        out_shape=jax.ShapeDtypeStruct((n, SSD_WIDTH), BF16),
        scratch_shapes=[pltpu.VMEM((8, XBC_WIDTH), F32),
                        pltpu.VMEM((D_STATE, SSD_WIDTH), F32)],
        compiler_params=_cparams("parallel"),
        name="ssd_mixer",
    )(z, xbc, misc, conv_w, conv_b, dtb_lane, alog_lane, dsk, norm_w, expand)


def _outproj_kernel(attn_ref, ssd_ref, x_ref, wa_ref, ws_ref, g_ref, b_ref, h_ref, hb_ref, hbt_ref, *, alpha):
    mix = _dot(attn_ref[...], wa_ref[...]) + _dot(ssd_ref[...], ws_ref[...])
    h = _layer_norm(alpha * x_ref[...] + mix, g_ref[...], b_ref[...])
    h_ref[...] = h
    hb_ref[...] = h.astype(BF16)
    hbt_ref[...] = h.T.astype(BF16)


def _outproj(attn, ssd, x2, wa, ws, g, b, alpha, tm):
    n, d = x2.shape
    row = lambda w: pl.BlockSpec((tm, w), lambda i: (i, 0))
    full = lambda a: pl.BlockSpec(a.shape, lambda i: (0, 0))
    return pl.pallas_call(
        functools.partial(_outproj_kernel, alpha=alpha),
        grid=(n // tm,),
        in_specs=[row(ATTN_WIDTH), row(SSD_WIDTH), row(d), full(wa), full(ws), full(g), full(b)],
        out_specs=[row(d), row(d), pl.BlockSpec((d, tm), lambda i: (0, i))],
        out_shape=[jax.ShapeDtypeStruct((n, d), F32), jax.ShapeDtypeStruct((n, d), BF16),
                   jax.ShapeDtypeStruct((d, n), BF16)],
        compiler_params=_cparams("parallel"),
        name="outproj_ln1",
    )(attn, ssd, x2, wa, ws, g, b)


def _peer_score_kernel(hb_ref, wq_ref, k1_ref, k2_ref, s1_ref, s2_ref):
    qb = _dot(hb_ref[...], wq_ref[...]).astype(BF16)
    k1 = k1_ref[...]
    k2 = k2_ref[...]
    for h in range(PEER_HEADS):
        o = h * 2 * PEER_HALF
        s1_ref[h] = _dot_nt(k1, qb[:, o:o + PEER_HALF])
        s2_ref[h] = _dot_nt(k2, qb[:, o + PEER_HALF:o + 2 * PEER_HALF])


def _peer_scores(hb, wq, k1, k2, tm):
    n, d = hb.shape
    full = lambda a: pl.BlockSpec(a.shape, lambda i: (0, 0))
    sblk = pl.BlockSpec((PEER_HEADS, N_KEYS, tm), lambda i: (0, 0, i))
    sshape = jax.ShapeDtypeStruct((PEER_HEADS, N_KEYS, n), F32)
    return pl.pallas_call(
        _peer_score_kernel,
        grid=(n // tm,),
        in_specs=[pl.BlockSpec((tm, d), lambda i: (i, 0)), full(wq), full(k1), full(k2)],
        out_specs=[sblk, sblk],
        out_shape=[sshape, sshape],
        compiler_params=_cparams("parallel"),
        name="peer_scores",
    )(hb, wq, k1, k2)


_CAND_COUNTS = [PEER_TOPK // (j1 + 1) for j1 in range(PEER_SUB_TOPK)]
_CAND_ROWS = sum(_CAND_COUNTS)
_CAND_PAD = -(-_CAND_ROWS // 8) * 8


def _bitonic_desc(xs, full):
    xs = list(xs)
    n = len(xs)
    k = 2 if full else n
    while k <= n:
        j = k // 2
        while j >= 1:
            for i in range(n):
                l = i ^ j
                if l > i:
                    hi, lo = jnp.maximum(xs[i], xs[l]), jnp.minimum(xs[i], xs[l])
                    xs[i], xs[l] = (hi, lo) if (i & k) == 0 else (lo, hi)
            j //= 2
        k *= 2
    return xs


def _peer_select_kernel(s1_ref, s2_ref, pos_ref, ea_ref, n1_ref, eb_ref, r2_ref,
                        v_scr, rank_scr, cand_scr, pick_scr):
    tl = s1_ref.shape[2]
    kio = lax.broadcasted_iota(jnp.int32, (N_KEYS, tl), 0).astype(F32)
    cpos = jnp.broadcast_to(pos_ref[...], (_CAND_PAD, tl))
    not_ranked = float(PEER_SUB_TOPK)

    def top_ranks(s, slot):
        rank = jnp.full((N_KEYS, tl), not_ranked, F32)
        for jj in range(PEER_SUB_TOPK):
            m = jnp.max(s, axis=0, keepdims=True)
            hit = kio == jnp.min(jnp.where(s == m, kio, float(N_KEYS)), axis=0, keepdims=True)
            s = jnp.where(hit, -jnp.inf, s)
            rank = jnp.where(hit, float(jj), rank)
            v_scr[slot, jj:jj + 1, :] = m
        rank_scr[slot] = rank

    def top_sorted(s, slot, want_rank):
        nv = N_KEYS // 8
        xs = _bitonic_desc([s[8 * i:8 * (i + 1), :] for i in range(nv)], full=True)
        for shift in (4, 2, 1):
            ys = [pltpu.roll(x, shift, 0) for x in xs]
            xs = _bitonic_desc([jnp.maximum(xs[i], ys[nv - 1 - i]) for i in range(nv)], full=False)
        xs = xs[:PEER_SUB_TOPK]
        for jj in range(PEER_SUB_TOPK):
            v_scr[slot, jj:jj + 1, :] = xs[jj][0:1, :]
        if want_rank:
            rank = jnp.concatenate(
                [sum(jnp.where(v > s[8 * i:8 * (i + 1), :], 1.0, 0.0) for v in xs) for i in range(nv)], axis=0)
            rank_scr[slot] = rank
            in_top = rank < not_ranked
        else:
            in_top = jnp.concatenate([s[8 * i:8 * (i + 1), :] >= xs[-1] for i in range(nv)], axis=0)
        marked = jnp.max(jnp.sum(jnp.where(in_top, 1.0, 0.0), axis=0, keepdims=True))
        equal = sum(jnp.where(xs[jj] == xs[jj + 1], 1.0, 0.0) for jj in range(PEER_SUB_TOPK - 1))
        return marked + jnp.max(equal)

    def pick_pairs(exact):
        cand = cand_scr[...]
        picked = jnp.zeros((_CAND_PAD, tl), F32)
        for kk in range(PEER_TOPK):
            m = jnp.max(cand, axis=0, keepdims=True)
            hit = cand == m
            if exact:
                hit = cpos == jnp.min(jnp.where(hit, cpos, 1e9), axis=0, keepdims=True)
            cand = jnp.where(hit, -jnp.inf, cand)
            picked = jnp.where(hit, 1.0, picked)
        pick_scr[...] = picked
        return jnp.max(jnp.sum(picked, axis=0, keepdims=True))

    def head(h, carry):
        s1 = s1_ref[h]
        s2 = s2_ref[h]
        marked = jnp.maximum(top_sorted(s1, 0, False), top_sorted(s2, 1, True))
        tied = marked > float(PEER_SUB_TOPK)

        @pl.when(tied)
        def _():
            top_ranks(s1, 0)
            top_ranks(s2, 1)

        v1 = v_scr[0]
        v2 = v_scr[1]
        o = 0
        for j1, cnt in enumerate(_CAND_COUNTS):
            cand_scr[o:o + cnt, :] = v1[j1:j1 + 1, :] + v2[0:cnt, :]
            o += cnt
        if _CAND_PAD > _CAND_ROWS:
            cand_scr[_CAND_ROWS:, :] = jnp.full((_CAND_PAD - _CAND_ROWS, tl), -jnp.inf, F32)
        npicked = pick_pairs(False)

        @pl.when(npicked > float(PEER_TOPK))
        def _():
            pick_pairs(True)

        picked = pick_scr[...]
        best0 = v1[0:1, :] + v2[0:1, :]
        zsum = jnp.sum(jnp.where(picked > 0.0, jnp.exp(cand_scr[...] - best0), 0.0), axis=0, keepdims=True)
        def staircase(is_rank_j1):
            n1 = jnp.zeros((N_KEYS, tl), F32)
            o = 0
            for j1, cnt in enumerate(_CAND_COUNTS):
                nsel = jnp.sum(pick_scr[o:o + cnt, :], axis=0, keepdims=True)
                n1 = jnp.where(is_rank_j1(j1), nsel, n1)
                o += cnt
            n1_ref[h] = n1

        @pl.when(jnp.logical_not(tied))
        def _():
            staircase(lambda j1: s1 == v1[j1:j1 + 1, :])

        @pl.when(tied)
        def _():
            rank1 = rank_scr[0]
            staircase(lambda j1: rank1 == float(j1))

        ea_ref[h] = jnp.exp(s1 - v1[0:1, :]) * (0.5 / zsum)
        eb_ref[h] = jnp.exp(s2 - v2[0:1, :]).astype(BF16)
        r2_ref[h] = rank_scr[1].astype(BF16)
        return carry

    lax.fori_loop(0, PEER_HEADS, head, 0)


def _peer_select(s1t, s2t, cand_pos, tl):
    n = s1t.shape[2]
    blk = pl.BlockSpec((PEER_HEADS, N_KEYS, tl), lambda i: (0, 0, i))
    shp = lambda dt: jax.ShapeDtypeStruct((PEER_HEADS, N_KEYS, n), dt)
    return pl.pallas_call(
        _peer_select_kernel,
        grid=(n // tl,),
        in_specs=[blk, blk, pl.BlockSpec(cand_pos.shape, lambda i: (0, 0))],
        out_specs=[blk, blk, blk, blk],
        out_shape=[shp(F32), shp(F32), shp(BF16), shp(BF16)],
        scratch_shapes=[pltpu.VMEM((2, PEER_SUB_TOPK, tl), F32),
                        pltpu.VMEM((2, N_KEYS, tl), F32),
                        pltpu.VMEM((_CAND_PAD, tl), F32),
                        pltpu.VMEM((_CAND_PAD, tl), F32)],
        compiler_params=_cparams("parallel"),
        name="peer_select",
    )(s1t, s2t, cand_pos)


PEER_TOK = 256
BF16_ROWS = 16


def _peer_expert_kernel(ht_ref, u_ref, vt_ref, ea_ref, n1_ref, eb_ref, r2_ref, o_ref,
                        act0_scr, act1_scr, acc_scr, pt0_scr, pt1_scr):
    e = pl.program_id(1)
    es = u_ref.shape[0]
    tl = ht_ref.shape[1]
    grp = N_KEYS // BF16_ROWS

    @pl.when(e == 0)
    def _():
        acc_scr[...] = jnp.zeros(acc_scr.shape, F32)
        act1_scr[...] = jnp.zeros(act1_scr.shape, BF16)

    def step(act_w, act_r):
        zero = jnp.zeros((), BF16)
        npc = tl // PEER_TOK

        pt_bufs = (pt0_scr, pt1_scr)

        def first_matmul(p):
            pt_bufs[p % 2][...] = _dot(u_ref[...], ht_ref[:, p * PEER_TOK:(p + 1) * PEER_TOK])

        def second_matmul(p):
            cols = slice(p * PEER_TOK, (p + 1) * PEER_TOK)
            acc_scr[:, cols] += _dot(vt_ref[...], act_r[:, cols])

        def gate_block(p):
            cols = slice(p * PEER_TOK, (p + 1) * PEER_TOK)
            pt = pt_bufs[p % 2]

            def rows16(ref, h, cc):
                return jnp.broadcast_to(ref[h, cc:cc + 1, cols], (BF16_ROWS, PEER_TOK)).astype(BF16)[None]

            for cc in range(es // N_KEYS):
                g = jnp.zeros((grp, BF16_ROWS, PEER_TOK), BF16)
                for h in range(PEER_HEADS):
                    r2 = r2_ref[h, :, cols].reshape(grp, BF16_ROWS, PEER_TOK)
                    eb = eb_ref[h, :, cols].reshape(grp, BF16_ROWS, PEER_TOK)
                    g = g + jnp.where(r2 < rows16(n1_ref, h, cc), eb, zero) * rows16(ea_ref, h, cc)
                x = pt[cc * N_KEYS:(cc + 1) * N_KEYS, :]
                gelu2 = x * (1.0 + lax.erf(x * (2.0 ** -0.5)))
                act_w[cc * N_KEYS:(cc + 1) * N_KEYS, cols] = gelu2.astype(BF16) * g.reshape(N_KEYS, PEER_TOK)


        first_matmul(0)
        for p in range(npc):
            if p + 1 < npc:
                first_matmul(p + 1)
            second_matmul(p)
            gate_block(p)

    @pl.when(lax.rem(e, 2) == 0)
    def _():
        step(act0_scr, act1_scr)

    @pl.when(lax.rem(e, 2) == 1)
    def _():
        step(act1_scr, act0_scr)

    @pl.when(e == pl.num_programs(1) - 1)
    def _():
        o_ref[...] = acc_scr[...].T


def _peer_experts(ht, u, vt, ea, n1, eb, r2, tl, es):
    d, n = ht.shape
    ng = u.shape[0] // es
    rows = es // N_KEYS
    cur_g = lambda e: jnp.minimum(e, ng - 1)
    a_blk = pl.BlockSpec((PEER_HEADS, rows, tl), lambda i, e: (0, cur_g(e), i))
    b_blk = pl.BlockSpec((PEER_HEADS, N_KEYS, tl), lambda i, e: (0, 0, i))
    return pl.pallas_call(
        _peer_expert_kernel,
        grid=(n // tl, ng + 1),
        in_specs=[pl.BlockSpec((d, tl), lambda i, e: (0, i)),
                  pl.BlockSpec((es, d), lambda i, e: (cur_g(e), 0)),
                  pl.BlockSpec((d, es), lambda i, e: (0, jnp.maximum(e - 1, 0))),
                  a_blk, a_blk, b_blk, b_blk],
        out_specs=pl.BlockSpec((tl, d), lambda i, e: (i, 0)),
        out_shape=jax.ShapeDtypeStruct((n, d), F32),
        scratch_shapes=[pltpu.VMEM((es, tl), BF16), pltpu.VMEM((es, tl), BF16), pltpu.VMEM((d, tl), F32),
                        pltpu.VMEM((es, PEER_TOK), F32), pltpu.VMEM((es, PEER_TOK), F32)],
        compiler_params=_cparams("parallel", "arbitrary"),
        name="peer_experts",
    )(ht, u, vt, ea, n1, eb, r2)


def _ln2_kernel(h_ref, f_ref, g_ref, b_ref, o_ref, *, alpha):
    o_ref[...] = _layer_norm(alpha * h_ref[...] + f_ref[...], g_ref[...], b_ref[...])


def _ln2(h, ffn, g, b, alpha, tm):
    n, d = h.shape
    row = pl.BlockSpec((tm, d), lambda i: (i, 0))
    full = lambda a: pl.BlockSpec(a.shape, lambda i: (0, 0))
    return pl.pallas_call(
        functools.partial(_ln2_kernel, alpha=alpha),
        grid=(n // tm,),
        in_specs=[row, row, full(g), full(b)],
        out_specs=row,
        out_shape=jax.ShapeDtypeStruct((n, d), F32),
        compiler_params=_cparams("parallel"),
        name="residual_ln2",
    )(h, ffn, g, b)


def _pick(n, pref):
    t = min(pref, n)
    while n % t:
        t //= 2
    return t


def _layer(x, positions, w_in, conv_w, conv_b, dt_bias, a_log, d_skip, ssm_norm_w, w_out,
           ln1_g, ln1_b, peer_wq, peer_k1, peer_k2, peer_u, peer_v, ln2_g, ln2_b, alpha):
    bsz, seq, d = x.shape
    n = bsz * seq
    x2 = x.reshape(n, d)
    pos2 = positions.reshape(n, 1)

    sizes = (ATTN_WIDTH, ATTN_WIDTH, ATTN_WIDTH, IDX_HEADS * IDX_DIM, IDX_DIM, IDX_HEADS,
             SSD_WIDTH, XBC_WIDTH, SSD_HEADS)
    offs = [0]
    for s in sizes:
        offs.append(offs[-1] + s)
    col = lambda i: w_in[:, offs[i]:offs[i + 1]]
    wv = jnp.pad(col(2).reshape(d, ATTN_HEADS, HEAD_DIM), ((0, 0), (0, 0), (0, LANES - HEAD_DIM)))
    w_main = jnp.concatenate([col(0), col(1), wv.reshape(d, ATTN_HEADS * LANES), col(3), col(6), col(7)],
                             axis=1).astype(BF16)
    w_misc = jnp.concatenate(
        [col(4), col(5), col(8), jnp.zeros((d, LANES - IDX_DIM - IDX_HEADS - SSD_HEADS), w_in.dtype)],
        axis=1).astype(BF16)
    lane = jnp.arange(LANES)
    inv64 = (ROPE_THETA ** (-jnp.arange(0, HEAD_DIM, 2, dtype=F32) / HEAD_DIM))[(lane % HEAD_DIM) % (HEAD_DIM // 2)]
    inv32 = (ROPE_THETA ** (-jnp.arange(0, IDX_DIM, 2, dtype=F32) / IDX_DIM))[(lane % IDX_DIM) % (IDX_DIM // 2)]

    q, k, va, iq, z, xbc, misc = _inproj(x2, pos2, w_main, w_misc, inv64[None, :], inv32[None, :], _pick(n, 512))

    attn = _dsa(q, k, va, iq, misc, bsz, seq, _pick(seq, 256))

    pad_dt = lambda a: jnp.zeros((1, LANES), F32).at[0, MISC_DT:MISC_DT + SSD_HEADS].set(a.astype(F32))
    expand = (jnp.arange(LANES)[:, None] == MISC_DT + jnp.arange(SSD_WIDTH)[None, :] // SSD_HEAD_DIM).astype(F32)
    ssd = _ssd(z, xbc, misc, conv_w.astype(F32), conv_b.astype(F32)[None, :], pad_dt(dt_bias), pad_dt(a_log),
               jnp.repeat(d_skip.astype(F32), SSD_HEAD_DIM)[None, :], ssm_norm_w.astype(F32)[None, :],
               expand, bsz, seq)

    wob = w_out.astype(BF16)
    h, hb, hbt = _outproj(attn, ssd, x2, wob[:ATTN_WIDTH], wob[ATTN_WIDTH:], ln1_g[None, :], ln1_b[None, :],
                          alpha, _pick(n, 512))

    s1t, s2t = _peer_scores(hb, peer_wq.astype(BF16), peer_k1.astype(BF16), peer_k2.astype(BF16), _pick(n, 512))

    cand_pos = []
    for j1, cnt in enumerate(_CAND_COUNTS):
        cand_pos += [float(j1 * PEER_SUB_TOPK + j2) for j2 in range(cnt)]
    cand_pos += [1e9] * (_CAND_PAD - _CAND_ROWS)
    ea, n1, eb, r2 = _peer_select(s1t, s2t, jnp.asarray(cand_pos, F32)[:, None], _pick(n, 512))

    ffn = _peer_experts(hbt, peer_u.astype(BF16), peer_v.T.astype(BF16), ea, n1, eb, r2,
                        _pick(n, 1024), 8 * N_KEYS)

    out = _ln2(h, ffn, ln2_g[None, :], ln2_b[None, :], alpha, _pick(n, 512))
    return out.reshape(bsz, seq, d)


def kernel(x, positions, w_in, conv_w, conv_b, dt_bias, a_log, d_skip, ssm_norm_w, w_out, ln1_g, ln1_b,
           peer_wq, peer_k1, peer_k2, peer_u, peer_v, ln2_g, ln2_b):
    depth = w_in.shape[0]
    alpha = float((2 * depth) ** 0.25)
    for i in range(depth):
        x = _layer(x, positions, w_in[i], conv_w[i], conv_b[i], dt_bias[i], a_log[i], d_skip[i],
                   ssm_norm_w[i], w_out[i], ln1_g[i], ln1_b[i], peer_wq[i], peer_k1[i], peer_k2[i],
                   peer_u[i], peer_v[i], ln2_g[i], ln2_b[i], alpha)
    return x
```

```python
import functools
import math

import jax
import jax.numpy as jnp
from jax import lax
from jax.experimental import pallas as pl
from jax.experimental.pallas import tpu as pltpu

F32 = jnp.float32
BF16 = jnp.bfloat16

ATTN_HEADS = 8
HEAD_DIM = 64
ATTN_WIDTH = ATTN_HEADS * HEAD_DIM
IDX_HEADS = 8
IDX_DIM = 32
MAX_TOPK_KEYS = 256
SSD_HEADS = 8
SSD_HEAD_DIM = 64
SSD_WIDTH = SSD_HEADS * SSD_HEAD_DIM
SSD_GROUPS = 2
D_STATE = 128
CONV_WIDTH = 4
CHUNK = 128
XBC_WIDTH = SSD_WIDTH + 2 * SSD_GROUPS * D_STATE
PEER_HEADS = 8
N_KEYS = 128
PEER_HALF = 128
PEER_SUB_TOPK = 16
PEER_TOPK = 16
ROPE_THETA = 10000.0
LN_EPS = 1e-5

MISC_IK = 0
MISC_IW = IDX_DIM
MISC_DT = IDX_DIM + IDX_HEADS
LANES = 128

F32_MIN_NORMAL = float(2.0 ** -126)
F32_MAX = float((2.0 - 2.0 ** -23) * 2.0 ** 127)
NEG_BIG = -1e30
ATTN_ROWS = 128

VMEM_LIMIT = 56 * 1024 * 1024


def _cparams(*sem):
    return pltpu.CompilerParams(dimension_semantics=sem, vmem_limit_bytes=VMEM_LIMIT)


def _dot(a, b):
    return jnp.dot(a, b, preferred_element_type=F32)


def _dot_nt(a, b):
    return lax.dot_general(a, b, (((1,), (1,)), ((), ())), preferred_element_type=F32)


def _dot_exact(a, b):
    return jnp.dot(a, b, preferred_element_type=F32, precision=lax.Precision.HIGHEST)


def _silu(x):
    return x / (1.0 + jnp.exp(-x))


def _softplus(x):
    return jnp.maximum(x, 0.0) + jnp.log(1.0 + jnp.exp(-jnp.abs(x)))


def _layer_norm(y, g, b):
    mu = jnp.mean(y, axis=-1, keepdims=True)
    d = y - mu
    var = jnp.mean(d * d, axis=-1, keepdims=True)
    return d * lax.rsqrt(var + LN_EPS) * g + b


def _swap_halves(x, half):
    w = x.shape[-1]
    lane = lax.broadcasted_iota(jnp.int32, x.shape, x.ndim - 1)
    first = (lane & (2 * half - 1)) < half
    return jnp.where(first, pltpu.roll(x, w - half, x.ndim - 1), pltpu.roll(x, half, x.ndim - 1))


def _inproj_kernel(x_ref, pos_ref, wm_ref, wx_ref, inv64_ref, inv32_ref,
                   q_ref, k_ref, va_ref, iq_ref, z_ref, xbc_ref, misc_ref):
    xb = x_ref[...].astype(BF16)
    pos = pos_ref[...].astype(F32)
    lane = lax.broadcasted_iota(jnp.int32, (1, LANES), 1)

    ang64 = pos * inv64_ref[...]
    cos64 = jnp.cos(ang64)
    sin64 = jnp.where((lane & (HEAD_DIM - 1)) < HEAD_DIM // 2, -1.0, 1.0) * jnp.sin(ang64)
    ang32 = pos * inv32_ref[...]
    cos32 = jnp.cos(ang32)
    sin32 = jnp.where((lane & (IDX_DIM - 1)) < IDX_DIM // 2, -1.0, 1.0) * jnp.sin(ang32)

    def rope(t, cos_t, sin_t, half):
        reps = t.shape[-1] // LANES
        c = jnp.concatenate([cos_t] * reps, axis=1) if reps > 1 else cos_t
        s = jnp.concatenate([sin_t] * reps, axis=1) if reps > 1 else sin_t
        return t * c + _swap_halves(t, half) * s

    o = 0
    q = _dot(xb, wm_ref[:, o:o + ATTN_WIDTH]); o += ATTN_WIDTH
    q_ref[...] = (rope(q, cos64, sin64, HEAD_DIM // 2) * (HEAD_DIM ** -0.5)).astype(BF16)
    k = _dot(xb, wm_ref[:, o:o + ATTN_WIDTH]); o += ATTN_WIDTH
    k_ref[...] = rope(k, cos64, sin64, HEAD_DIM // 2).astype(BF16)
    va = _dot(xb, wm_ref[:, o:o + ATTN_HEADS * LANES]); o += ATTN_HEADS * LANES
    pad = lax.broadcasted_iota(jnp.int32, (1, ATTN_HEADS * LANES), 1) & (LANES - 1)
    va_ref[...] = jnp.where(pad >= HEAD_DIM, 1.0, va).astype(BF16)
    iq = _dot(xb, wm_ref[:, o:o + IDX_HEADS * IDX_DIM]); o += IDX_HEADS * IDX_DIM
    iq_ref[...] = rope(iq, cos32, sin32, IDX_DIM // 2).astype(BF16)
    z_ref[...] = _dot(xb, wm_ref[:, o:o + SSD_WIDTH]).astype(BF16); o += SSD_WIDTH
    xbc_ref[...] = _dot(xb, wm_ref[:, o:o + XBC_WIDTH]).astype(BF16)
    misc = _dot(xb, wx_ref[...])
    misc_ref[...] = jnp.where(lane < IDX_DIM, rope(misc, cos32, sin32, IDX_DIM // 2), misc)


def _inproj(x2, pos2, w_main, w_misc, inv64, inv32, tm):
    n, d = x2.shape
    wm = w_main.shape[1]
    row = lambda w: pl.BlockSpec((tm, w), lambda i: (i, 0))
    full = lambda a: pl.BlockSpec(a.shape, lambda i: (0, 0))
    outs = [(ATTN_WIDTH, BF16), (ATTN_WIDTH, BF16), (ATTN_HEADS * LANES, BF16),
            (IDX_HEADS * IDX_DIM, BF16), (SSD_WIDTH, BF16), (XBC_WIDTH, BF16), (LANES, F32)]
    return pl.pallas_call(
        _inproj_kernel,
        grid=(n // tm,),
        in_specs=[row(d), row(1), full(w_main), full(w_misc), full(inv64), full(inv32)],
        out_specs=[row(w) for w, _ in outs],
        out_shape=[jax.ShapeDtypeStruct((n, w), dt) for w, dt in outs],
        compiler_params=_cparams("parallel"),
        name="inproj_rope",
    )(x2, pos2, w_main, w_misc, inv64, inv32)


def _dsa_kernel(q_ref, k_ref, va_ref, iq_ref, mq_ref, mk_ref, o_ref,
                s_scr, m_scr, acc_scr, lg_scr, p_scr, a_scr, *, topk, tq):
    tk = tq
    j = pl.program_id(1)
    nch = j + 1
    kf = float(topk)

    iw_t = mq_ref[...].T[MISC_IW:MISC_IW + IDX_HEADS, :] * (1.0 / 16.0)
    iq = iq_ref[...]
    qpos = j * tq + lax.broadcasted_iota(jnp.int32, (tk, tq), 1)
    kio = lax.broadcasted_iota(jnp.int32, (tk, tq), 0)

    def score_chunk(c, carry):
        off = pl.multiple_of(c * tk, tk)
        ikc = mk_ref[pl.ds(off, tk), MISC_IK:MISC_IK + IDX_DIM].astype(BF16)
        for h in range(IDX_HEADS):
            lg_scr[h] = _dot_nt(ikc, iq[:, h * IDX_DIM:(h + 1) * IDX_DIM])
        sc = jnp.zeros((tk, tq), F32)
        for h in range(IDX_HEADS):
            sc = sc + iw_t[h:h + 1, :] * jnp.maximum(lg_scr[h], 0.0)
        s_scr[c] = jnp.where(c * tk + kio <= qpos, sc, -jnp.inf)
        return carry

    lax.fori_loop(0, nch, score_chunk, 0)

    def count(pred):
        def body(c, acc):
            s3 = s_scr[c].reshape(tk // 8, 8, tq)
            return acc + jnp.sum(jnp.where(pred(s3, c), 1.0, 0.0), axis=0)
        acc = lax.fori_loop(0, nch, body, jnp.zeros((8, tq), F32))
        return jnp.sum(acc, axis=0, keepdims=True)

    def rows8(t):
        return jnp.broadcast_to(t, (8, tq))[None]

    def count_ge(t):
        t8 = rows8(t)
        return count(lambda s3, c: s3 >= t8)

    pos_row = count_ge(jnp.zeros((1, tq), F32)) >= kf
    sgn = jnp.where(pos_row, 1.0, -1.0)

    def accept(trial_mag):
        ok = jnp.where(count_ge(sgn * trial_mag) >= kf, 1.0, -1.0)
        return ok == sgn

    mag = jnp.where(accept(jnp.full((1, tq), F32_MIN_NORMAL, F32)), F32_MIN_NORMAL, 0.0)
    for b in range(7, -1, -1):
        if b == 7:
            trial = (mag * float(2.0 ** 64)) * float(2.0 ** 64)
        else:
            trial = mag * float(2.0 ** (2 ** b))
        trial = jnp.minimum(trial, F32_MAX)
        mag = jnp.where(accept(trial), trial, mag)
    base = mag
    for kbit in range(1, 24):
        trial = jnp.minimum(mag + base * float(2.0 ** -kbit), F32_MAX)
        mag = jnp.where(accept(trial), trial, mag)
    ulp = jnp.where(base > 0.0, base * float(2.0 ** -23), F32_MIN_NORMAL)
    thr = jnp.where(pos_row, mag, -(mag + ulp))

    n_ge = count_ge(thr)

    @pl.when(jnp.max(n_ge) > kf)
    def _():
        t8 = rows8(thr)
        kio3 = kio.reshape(tk // 8, 8, tq)
        n_gt = count(lambda s3, c: s3 > t8)
        n_eq = n_ge - n_gt
        target = jnp.minimum(kf - n_gt, n_eq)
        mpos = jnp.zeros((1, tq), F32)
        nbits = int(math.ceil(math.log2(tk * s_scr.shape[0]))) + 1
        for b in range(nbits - 1, -1, -1):
            trial = mpos + float(2 ** b)
            tr8 = rows8(trial)
            cnt = count(lambda s3, c: jnp.where(s3 == t8, (c * tk + kio3).astype(F32), 1e9) < tr8)
            mpos = jnp.where(cnt < target, trial, mpos)

        def drop(c, carry):
            s = s_scr[c]
            kill = jnp.where(s == thr, (c * tk + kio).astype(F32), -1.0) > mpos
            s_scr[c] = jnp.where(kill, -jnp.inf, s)
            return carry

        lax.fori_loop(0, nch, drop, 0)

    def wide(t):
        return jnp.concatenate([t] * (tk // LANES), axis=1)

    thr_q = jnp.broadcast_to(jnp.maximum(thr, -F32_MAX), (LANES, tq)).T
    thr_w = wide(thr_q)
    m_scr[...] = jnp.full(m_scr.shape, NEG_BIG, F32)
    acc_scr[...] = jnp.zeros(acc_scr.shape, F32)

    def attn_chunk(c, carry):
        off = pl.multiple_of(c * tk, tk)
        sel = s_scr[c].T >= thr_w
        for h in range(ATTN_HEADS):
            hs = slice(h * HEAD_DIM, (h + 1) * HEAD_DIM)
            lg_scr[h] = _dot_nt(q_ref[:, hs], k_ref[pl.ds(off, tk), hs])
        for h in range(ATTN_HEADS):
            lg = jnp.where(sel, lg_scr[h], -jnp.inf)
            m_old = m_scr[h]
            m_new = jnp.maximum(m_old, jnp.max(lg, axis=1, keepdims=True))
            p_scr[h] = jnp.exp(lg - wide(m_new)).astype(BF16)
            a_scr[h] = jnp.exp(m_old - m_new)
            m_scr[h] = m_new
        for h in range(ATTN_HEADS):
            pv = _dot(p_scr[h], va_ref[pl.ds(off, tk), h * LANES:(h + 1) * LANES])
            acc_scr[h] = a_scr[h] * acc_scr[h] + pv
        return carry

    lax.fori_loop(0, nch, attn_chunk, 0)
    for h in range(ATTN_HEADS):
        a = acc_scr[h]
        o_ref[:, h * HEAD_DIM:(h + 1) * HEAD_DIM] = (a[:, :HEAD_DIM] / a[:, HEAD_DIM:]).astype(BF16)


def _dsa(q, k, va, iq, misc, bsz, seq, tq):
    n = bsz * seq
    nq = seq // tq
    topk = min(MAX_TOPK_KEYS, seq // 4)
    qblk = lambda w: pl.BlockSpec((tq, w), lambda b, j: (b * nq + j, 0))
    kblk = lambda w: pl.BlockSpec((seq, w), lambda b, j: (b, 0))
    return pl.pallas_call(
        functools.partial(_dsa_kernel, topk=topk, tq=tq),
        grid=(bsz, nq),
        in_specs=[qblk(ATTN_WIDTH), kblk(ATTN_WIDTH), kblk(ATTN_HEADS * LANES),
                  qblk(IDX_HEADS * IDX_DIM), qblk(LANES), kblk(LANES)],
        out_specs=qblk(ATTN_WIDTH),
        out_shape=jax.ShapeDtypeStruct((n, ATTN_WIDTH), BF16),
        scratch_shapes=[
            pltpu.VMEM((nq, tq, tq), F32),
            pltpu.VMEM((ATTN_HEADS, tq, LANES), F32),
            pltpu.VMEM((ATTN_HEADS, tq, LANES), F32),
            pltpu.VMEM((ATTN_HEADS, tq, tq), F32),
            pltpu.VMEM((ATTN_HEADS, tq, tq), BF16),
            pltpu.VMEM((ATTN_HEADS, tq, LANES), F32),
        ],
        compiler_params=_cparams("parallel", "arbitrary"),
        name="dsa_attention",
    )(q, k, va, iq, misc, misc)


def _ssd_kernel(z_ref, xbc_ref, misc_ref, cw_ref, cb_ref, dtb_ref, alog_ref, dsk_ref, nw_ref, e_ref,
                o_ref, tail_scr, st_scr):
    seq = z_ref.shape[0]
    t = CHUNK
    tail_scr[...] = jnp.zeros(tail_scr.shape, F32)
    st_scr[...] = jnp.zeros(st_scr.shape, F32)
    a_lane = -jnp.exp(alog_ref[...])
    r = lax.broadcasted_iota(jnp.int32, (t, t), 0)
    cidx = lax.broadcasted_iota(jnp.int32, (t, t), 1)
    lower = r >= cidx
    ltri = jnp.where(lower, 1.0, 0.0)
    expand = e_ref[...]
    gw = SSD_WIDTH // SSD_GROUPS
    hpg = SSD_HEADS // SSD_GROUPS

    def chunk(c, carry):
        off = pl.multiple_of(c * t, t)
        xin = xbc_ref[pl.ds(off, t), :].astype(F32)
        xp = jnp.concatenate([tail_scr[...], xin], axis=0)
        tail_scr[...] = xin[t - 8:, :]
        acc = jnp.broadcast_to(cb_ref[...], (t, XBC_WIDTH))
        for w in range(CONV_WIDTH):
            s0 = 8 - (CONV_WIDTH - 1) + w
            acc = acc + xp[s0:s0 + t, :] * cw_ref[w:w + 1, :]
        xc = _silu(acc)
        xs = xc[:, :SSD_WIDTH]
        bm = xc[:, SSD_WIDTH:SSD_WIDTH + SSD_GROUPS * D_STATE]
        cm = xc[:, SSD_WIDTH + SSD_GROUPS * D_STATE:]

        dt = _softplus(misc_ref[pl.ds(off, t), :] + dtb_ref[...])
        acum = _dot_exact(ltri, dt * a_lane)
        acum_t = acum.T
        dt_e = _dot_exact(dt, expand)
        ac_e = _dot_exact(acum, expand)
        alast_e = ac_e[t - 1:t, :]
        xdt = xs * dt_e
        xw = (xdt * jnp.exp(alast_e - ac_e)).astype(BF16)
        eac = jnp.exp(ac_e)
        xdt_b = xdt.astype(BF16)

        ys = []
        for g in range(SSD_GROUPS):
            gs = slice(g * gw, (g + 1) * gw)
            bg = bm[:, g * D_STATE:(g + 1) * D_STATE]
            cg = cm[:, g * D_STATE:(g + 1) * D_STATE].astype(BF16)
            cb = _dot_nt(cg, bg.astype(BF16))
            st_old = st_scr[:, gs]
            y_off = _dot(cg, st_old.astype(BF16)) * eac[:, gs]
            st_scr[:, gs] = jnp.exp(alast_e[:, gs]) * st_old + _dot(bg.T.astype(BF16), xw[:, gs])
            for hh in range(hpg):
                h = g * hpg + hh
                col = acum[:, MISC_DT + h:MISC_DT + h + 1]
                row = acum_t[MISC_DT + h:MISC_DT + h + 1, :]
                decay = jnp.exp(jnp.where(lower, col - row, -jnp.inf))
                wmat = (cb * decay).astype(BF16)
                hs = slice(h * SSD_HEAD_DIM, (h + 1) * SSD_HEAD_DIM)
                ys.append(_dot(wmat, xdt_b[:, hs]) + y_off[:, hh * SSD_HEAD_DIM:(hh + 1) * SSD_HEAD_DIM])
        y = jnp.concatenate(ys, axis=1) + xs * dsk_ref[...]
        y = y * _silu(z_ref[pl.ds(off, t), :].astype(F32))
        outs = []
        for g in range(SSD_GROUPS):
            yg = y[:, g * gw:(g + 1) * gw]
            outs.append(yg * lax.rsqrt(jnp.mean(yg * yg, axis=-1, keepdims=True) + LN_EPS))
        o_ref[pl.ds(off, t), :] = (jnp.concatenate(outs, axis=1) * nw_ref[...]).astype(BF16)
        return carry

    lax.fori_loop(0, seq // t, chunk, 0)


def _ssd(z, xbc, misc, conv_w, conv_b, dtb_lane, alog_lane, dsk, norm_w, expand, bsz, seq):
    n = bsz * seq
    blk = lambda w: pl.BlockSpec((seq, w), lambda b: (b, 0))
    full = lambda a: pl.BlockSpec(a.shape, lambda b: (0, 0))
    return pl.pallas_call(
        _ssd_kernel,
        grid=(bsz,),
        in_specs=[blk(SSD_WIDTH), blk(XBC_WIDTH), blk(LANES), full(conv_w), full(conv_b),
                  full(dtb_lane), full(alog_lane), full(dsk), full(norm_w), full(expand)],
        out_specs=blk(SSD_WIDTH),
        out_shape=jax.ShapeDtypeStruct((n, SSD_WIDTH), BF16),
        scratch_shapes=[pltpu.VMEM((8, XBC_WIDTH), F32),
                        pltpu.VMEM((D_STATE, SSD_WIDTH), F32)],
        compiler_params=_cparams("parallel"),
        name="ssd_mixer",
    )(z, xbc, misc, conv_w, conv_b, dtb_lane, alog_lane, dsk, norm_w, expand)


def _outproj_kernel(attn_ref, ssd_ref, x_ref, wa_ref, ws_ref, g_ref, b_ref, h_ref, hb_ref, hbt_ref, *, alpha):
    mix = _dot(attn_ref[...], wa_ref[...]) + _dot(ssd_ref[...], ws_ref[...])
    h = _layer_norm(alpha * x_ref[...] + mix, g_ref[...], b_ref[...])
    h_ref[...] = h
    hb_ref[...] = h.astype(BF16)
    hbt_ref[...] = h.T.astype(BF16)


def _outproj(attn, ssd, x2, wa, ws, g, b, alpha, tm):
    n, d = x2.shape
    row = lambda w: pl.BlockSpec((tm, w), lambda i: (i, 0))
    full = lambda a: pl.BlockSpec(a.shape, lambda i: (0, 0))
    return pl.pallas_call(
        functools.partial(_outproj_kernel, alpha=alpha),
        grid=(n // tm,),
        in_specs=[row(ATTN_WIDTH), row(SSD_WIDTH), row(d), full(wa), full(ws), full(g), full(b)],
        out_specs=[row(d), row(d), pl.BlockSpec((d, tm), lambda i: (0, i))],
        out_shape=[jax.ShapeDtypeStruct((n, d), F32), jax.ShapeDtypeStruct((n, d), BF16),
                   jax.ShapeDtypeStruct((d, n), BF16)],
        compiler_params=_cparams("parallel"),
        name="outproj_ln1",
    )(attn, ssd, x2, wa, ws, g, b)


def _peer_score_kernel(hb_ref, wq_ref, k1_ref, k2_ref, s1_ref, s2_ref):
    qb = _dot(hb_ref[...], wq_ref[...]).astype(BF16)
    k1 = k1_ref[...]
    k2 = k2_ref[...]
    for h in range(PEER_HEADS):
        o = h * 2 * PEER_HALF
        s1_ref[h] = _dot_nt(k1, qb[:, o:o + PEER_HALF])
        s2_ref[h] = _dot_nt(k2, qb[:, o + PEER_HALF:o + 2 * PEER_HALF])


def _peer_scores(hb, wq, k1, k2, tm):
    n, d = hb.shape
    full = lambda a: pl.BlockSpec(a.shape, lambda i: (0, 0))
    sblk = pl.BlockSpec((PEER_HEADS, N_KEYS, tm), lambda i: (0, 0, i))
    sshape = jax.ShapeDtypeStruct((PEER_HEADS, N_KEYS, n), F32)
    return pl.pallas_call(
        _peer_score_kernel,
        grid=(n // tm,),
        in_specs=[pl.BlockSpec((tm, d), lambda i: (i, 0)), full(wq), full(k1), full(k2)],
        out_specs=[sblk, sblk],
        out_shape=[sshape, sshape],
        compiler_params=_cparams("parallel"),
        name="peer_scores",
    )(hb, wq, k1, k2)


_CAND_COUNTS = [PEER_TOPK // (j1 + 1) for j1 in range(PEER_SUB_TOPK)]
_CAND_ROWS = sum(_CAND_COUNTS)
_CAND_PAD = -(-_CAND_ROWS // 8) * 8


def _bitonic_desc(xs, full):
    xs = list(xs)
    n = len(xs)
    k = 2 if full else n
    while k <= n:
        j = k // 2
        while j >= 1:
            for i in range(n):
                l = i ^ j
                if l > i:
                    hi, lo = jnp.maximum(xs[i], xs[l]), jnp.minimum(xs[i], xs[l])
                    xs[i], xs[l] = (hi, lo) if (i & k) == 0 else (lo, hi)
            j //= 2
        k *= 2
    return xs


def _peer_select_kernel(s1_ref, s2_ref, pos_ref, ea_ref, n1_ref, eb_ref, r2_ref,
                        v_scr, rank_scr, cand_scr, pick_scr):
    tl = s1_ref.shape[2]
    kio = lax.broadcasted_iota(jnp.int32, (N_KEYS, tl), 0).astype(F32)
    cpos = jnp.broadcast_to(pos_ref[...], (_CAND_PAD, tl))
    not_ranked = float(PEER_SUB_TOPK)

    def top_ranks(s, slot):
        rank = jnp.full((N_KEYS, tl), not_ranked, F32)
        for jj in range(PEER_SUB_TOPK):
            m = jnp.max(s, axis=0, keepdims=True)
            hit = kio == jnp.min(jnp.where(s == m, kio, float(N_KEYS)), axis=0, keepdims=True)
            s = jnp.where(hit, -jnp.inf, s)
            rank = jnp.where(hit, float(jj), rank)
            v_scr[slot, jj:jj + 1, :] = m
        rank_scr[slot] = rank

    def top_sorted(s, slot, want_rank):
        nv = N_KEYS // 8
        xs = _bitonic_desc([s[8 * i:8 * (i + 1), :] for i in range(nv)], full=True)
        for shift in (4, 2, 1):
            ys = [pltpu.roll(x, shift, 0) for x in xs]
            xs = _bitonic_desc([jnp.maximum(xs[i], ys[nv - 1 - i]) for i in range(nv)], full=False)
        xs = xs[:PEER_SUB_TOPK]
        for jj in range(PEER_SUB_TOPK):
            v_scr[slot, jj:jj + 1, :] = xs[jj][0:1, :]
        if want_rank:
            rank = jnp.concatenate(
                [sum(jnp.where(v > s[8 * i:8 * (i + 1), :], 1.0, 0.0) for v in xs) for i in range(nv)], axis=0)
            rank_scr[slot] = rank
            in_top = rank < not_ranked
        else:
            in_top = jnp.concatenate([s[8 * i:8 * (i + 1), :] >= xs[-1] for i in range(nv)], axis=0)
        marked = jnp.max(jnp.sum(jnp.where(in_top, 1.0, 0.0), axis=0, keepdims=True))
        equal = sum(jnp.where(xs[jj] == xs[jj + 1], 1.0, 0.0) for jj in range(PEER_SUB_TOPK - 1))
        return marked + jnp.max(equal)

    def pick_pairs(exact):
        cand = cand_scr[...]
        picked = jnp.zeros((_CAND_PAD, tl), F32)
        for kk in range(PEER_TOPK):
            m = jnp.max(cand, axis=0, keepdims=True)
            hit = cand == m
            if exact:
                hit = cpos == jnp.min(jnp.where(hit, cpos, 1e9), axis=0, keepdims=True)
            cand = jnp.where(hit, -jnp.inf, cand)
            picked = jnp.where(hit, 1.0, picked)
        pick_scr[...] = picked
        return jnp.max(jnp.sum(picked, axis=0, keepdims=True))

    def head(h, carry):
        s1 = s1_ref[h]
        s2 = s2_ref[h]
        marked = jnp.maximum(top_sorted(s1, 0, False), top_sorted(s2, 1, True))
        tied = marked > float(PEER_SUB_TOPK)

        @pl.when(tied)
        def _():
            top_ranks(s1, 0)
            top_ranks(s2, 1)

        v1 = v_scr[0]
        v2 = v_scr[1]
        o = 0
        for j1, cnt in enumerate(_CAND_COUNTS):
            cand_scr[o:o + cnt, :] = v1[j1:j1 + 1, :] + v2[0:cnt, :]
            o += cnt
        if _CAND_PAD > _CAND_ROWS:
            cand_scr[_CAND_ROWS:, :] = jnp.full((_CAND_PAD - _CAND_ROWS, tl), -jnp.inf, F32)
        npicked = pick_pairs(False)

        @pl.when(npicked > float(PEER_TOPK))
        def _():
            pick_pairs(True)

        picked = pick_scr[...]
        best0 = v1[0:1, :] + v2[0:1, :]
        zsum = jnp.sum(jnp.where(picked > 0.0, jnp.exp(cand_scr[...] - best0), 0.0), axis=0, keepdims=True)
        def staircase(is_rank_j1):
            n1 = jnp.zeros((N_KEYS, tl), F32)
            o = 0
            for j1, cnt in enumerate(_CAND_COUNTS):
                nsel = jnp.sum(pick_scr[o:o + cnt, :], axis=0, keepdims=True)
                n1 = jnp.where(is_rank_j1(j1), nsel, n1)
                o += cnt
            n1_ref[h] = n1

        @pl.when(jnp.logical_not(tied))
        def _():
            staircase(lambda j1: s1 == v1[j1:j1 + 1, :])

        @pl.when(tied)
        def _():
            rank1 = rank_scr[0]
            staircase(lambda j1: rank1 == float(j1))

        ea_ref[h] = jnp.exp(s1 - v1[0:1, :]) * (0.5 / zsum)
        eb_ref[h] = jnp.exp(s2 - v2[0:1, :]).astype(BF16)
        r2_ref[h] = rank_scr[1].astype(BF16)
        return carry

    lax.fori_loop(0, PEER_HEADS, head, 0)


def _peer_select(s1t, s2t, cand_pos, tl):
    n = s1t.shape[2]
    blk = pl.BlockSpec((PEER_HEADS, N_KEYS, tl), lambda i: (0, 0, i))
    shp = lambda dt: jax.ShapeDtypeStruct((PEER_HEADS, N_KEYS, n), dt)
    return pl.pallas_call(
        _peer_select_kernel,
        grid=(n // tl,),
        in_specs=[blk, blk, pl.BlockSpec(cand_pos.shape, lambda i: (0, 0))],
        out_specs=[blk, blk, blk, blk],
        out_shape=[shp(F32), shp(F32), shp(BF16), shp(BF16)],
        scratch_shapes=[pltpu.VMEM((2, PEER_SUB_TOPK, tl), F32),
                        pltpu.VMEM((2, N_KEYS, tl), F32),
                        pltpu.VMEM((_CAND_PAD, tl), F32),
                        pltpu.VMEM((_CAND_PAD, tl), F32)],
        compiler_params=_cparams("parallel"),
        name="peer_select",
    )(s1t, s2t, cand_pos)


PEER_TOK = 256
BF16_ROWS = 16


def _peer_expert_kernel(ht_ref, u_ref, vt_ref, vtl_ref, ea_ref, n1_ref, eb_ref, r2_ref, o_ref,
                        act0_scr, act1_scr, acc_scr, pt0_scr, pt1_scr):
    e = pl.program_id(1)
    es = u_ref.shape[0]
    tl = ht_ref.shape[1]
    grp = N_KEYS // BF16_ROWS

    @pl.when(e == 0)
    def _():
        acc_scr[...] = jnp.zeros(acc_scr.shape, F32)
        act1_scr[...] = jnp.zeros(act1_scr.shape, BF16)

    def step(act_w, act_r):
        zero = jnp.zeros((), BF16)
        npc = tl // PEER_TOK

        pt_bufs = (pt0_scr, pt1_scr)

        def first_matmul(p):
            pt_bufs[p % 2][...] = _dot(u_ref[...], ht_ref[:, p * PEER_TOK:(p + 1) * PEER_TOK])

        def second_matmul(p):
            cols = slice(p * PEER_TOK, (p + 1) * PEER_TOK)
            acc_scr[:, cols] += _dot(vt_ref[...], act_r[:, cols])

        def gate_block(p):
            cols = slice(p * PEER_TOK, (p + 1) * PEER_TOK)
            pt = pt_bufs[p % 2]

            def rows16(ref, h, cc):
                return jnp.broadcast_to(ref[h, cc:cc + 1, cols], (BF16_ROWS, PEER_TOK)).astype(BF16)[None]

            for cc in range(es // N_KEYS):
                g = jnp.zeros((grp, BF16_ROWS, PEER_TOK), BF16)
                for h in range(PEER_HEADS):
                    r2 = r2_ref[h, :, cols].reshape(grp, BF16_ROWS, PEER_TOK)
                    eb = eb_ref[h, :, cols].reshape(grp, BF16_ROWS, PEER_TOK)
                    g = g + jnp.where(r2 < rows16(n1_ref, h, cc), eb, zero) * rows16(ea_ref, h, cc)
                x = pt[cc * N_KEYS:(cc + 1) * N_KEYS, :]
                gelu2 = x * (1.0 + lax.erf(x * (2.0 ** -0.5)))
                act_w[cc * N_KEYS:(cc + 1) * N_KEYS, cols] = gelu2.astype(BF16) * g.reshape(N_KEYS, PEER_TOK)


        first_matmul(0)
        for p in range(npc):
            if p + 1 < npc:
                first_matmul(p + 1)
            second_matmul(p)
            gate_block(p)

    last = e == pl.num_programs(1) - 1

    def run(act_w, act_r):
        step(act_w, act_r)

        @pl.when(last)
        def _():
            o_ref[...] = (acc_scr[...] + _dot(vtl_ref[...], act_w[...])).T

    @pl.when(lax.rem(e, 2) == 0)
    def _():
        run(act0_scr, act1_scr)

    @pl.when(lax.rem(e, 2) == 1)
    def _():
        run(act1_scr, act0_scr)


def _peer_experts(ht, u, vt, ea, n1, eb, r2, tl, es):
    d, n = ht.shape
    ng = u.shape[0] // es
    rows = es // N_KEYS
    a_blk = pl.BlockSpec((PEER_HEADS, rows, tl), lambda i, e: (0, e, i))
    b_blk = pl.BlockSpec((PEER_HEADS, N_KEYS, tl), lambda i, e: (0, 0, i))
    return pl.pallas_call(
        _peer_expert_kernel,
        grid=(n // tl, ng),
        in_specs=[pl.BlockSpec((d, tl), lambda i, e: (0, i)),
                  pl.BlockSpec((es, d), lambda i, e: (e, 0)),
                  pl.BlockSpec((d, es), lambda i, e: (0, jnp.maximum(e - 1, 0))),
                  pl.BlockSpec((d, es), lambda i, e: (0, jnp.where(e == ng - 1, ng - 1, 0))),
                  a_blk, a_blk, b_blk, b_blk],
        out_specs=pl.BlockSpec((tl, d), lambda i, e: (i, 0)),
        out_shape=jax.ShapeDtypeStruct((n, d), F32),
        scratch_shapes=[pltpu.VMEM((es, tl), BF16), pltpu.VMEM((es, tl), BF16), pltpu.VMEM((d, tl), F32),
                        pltpu.VMEM((es, PEER_TOK), F32), pltpu.VMEM((es, PEER_TOK), F32)],
        compiler_params=_cparams("parallel", "arbitrary"),
        name="peer_experts",
    )(ht, u, vt, vt, ea, n1, eb, r2)


def _ln2_kernel(h_ref, f_ref, g_ref, b_ref, o_ref, *, alpha):
    o_ref[...] = _layer_norm(alpha * h_ref[...] + f_ref[...], g_ref[...], b_ref[...])


def _ln2(h, ffn, g, b, alpha, tm):
    n, d = h.shape
    row = pl.BlockSpec((tm, d), lambda i: (i, 0))
    full = lambda a: pl.BlockSpec(a.shape, lambda i: (0, 0))
    return pl.pallas_call(
        functools.partial(_ln2_kernel, alpha=alpha),
        grid=(n // tm,),
        in_specs=[row, row, full(g), full(b)],
        out_specs=row,
        out_shape=jax.ShapeDtypeStruct((n, d), F32),
        compiler_params=_cparams("parallel"),
        name="residual_ln2",
    )(h, ffn, g, b)


def _pick(n, pref):
    t = min(pref, n)
    while n % t:
        t //= 2
    return t


def _layer(x, positions, w_in, conv_w, conv_b, dt_bias, a_log, d_skip, ssm_norm_w, w_out,
           ln1_g, ln1_b, peer_wq, peer_k1, peer_k2, peer_u, peer_v, ln2_g, ln2_b, alpha):
    bsz, seq, d = x.shape
    n = bsz * seq
    x2 = x.reshape(n, d)
    pos2 = positions.reshape(n, 1)

    sizes = (ATTN_WIDTH, ATTN_WIDTH, ATTN_WIDTH, IDX_HEADS * IDX_DIM, IDX_DIM, IDX_HEADS,
             SSD_WIDTH, XBC_WIDTH, SSD_HEADS)
    offs = [0]
    for s in sizes:
        offs.append(offs[-1] + s)
    col = lambda i: w_in[:, offs[i]:offs[i + 1]]
    wv = jnp.pad(col(2).reshape(d, ATTN_HEADS, HEAD_DIM), ((0, 0), (0, 0), (0, LANES - HEAD_DIM)))
    w_main = jnp.concatenate([col(0), col(1), wv.reshape(d, ATTN_HEADS * LANES), col(3), col(6), col(7)],
                             axis=1).astype(BF16)
    w_misc = jnp.concatenate(
        [col(4), col(5), col(8), jnp.zeros((d, LANES - IDX_DIM - IDX_HEADS - SSD_HEADS), w_in.dtype)],
        axis=1).astype(BF16)
    lane = jnp.arange(LANES)
    inv64 = (ROPE_THETA ** (-jnp.arange(0, HEAD_DIM, 2, dtype=F32) / HEAD_DIM))[(lane % HEAD_DIM) % (HEAD_DIM // 2)]
    inv32 = (ROPE_THETA ** (-jnp.arange(0, IDX_DIM, 2, dtype=F32) / IDX_DIM))[(lane % IDX_DIM) % (IDX_DIM // 2)]

    q, k, va, iq, z, xbc, misc = _inproj(x2, pos2, w_main, w_misc, inv64[None, :], inv32[None, :], _pick(n, 512))

    attn = _dsa(q, k, va, iq, misc, bsz, seq, _pick(seq, 256))

    pad_dt = lambda a: jnp.zeros((1, LANES), F32).at[0, MISC_DT:MISC_DT + SSD_HEADS].set(a.astype(F32))
    expand = (jnp.arange(LANES)[:, None] == MISC_DT + jnp.arange(SSD_WIDTH)[None, :] // SSD_HEAD_DIM).astype(F32)
    ssd = _ssd(z, xbc, misc, conv_w.astype(F32), conv_b.astype(F32)[None, :], pad_dt(dt_bias), pad_dt(a_log),
               jnp.repeat(d_skip.astype(F32), SSD_HEAD_DIM)[None, :], ssm_norm_w.astype(F32)[None, :],
               expand, bsz, seq)

    wob = w_out.astype(BF16)
    h, hb, hbt = _outproj(attn, ssd, x2, wob[:ATTN_WIDTH], wob[ATTN_WIDTH:], ln1_g[None, :], ln1_b[None, :],
                          alpha, _pick(n, 512))

    s1t, s2t = _peer_scores(hb, peer_wq.astype(BF16), peer_k1.astype(BF16), peer_k2.astype(BF16), _pick(n, 512))

    cand_pos = []
    for j1, cnt in enumerate(_CAND_COUNTS):
        cand_pos += [float(j1 * PEER_SUB_TOPK + j2) for j2 in range(cnt)]
    cand_pos += [1e9] * (_CAND_PAD - _CAND_ROWS)
    ea, n1, eb, r2 = _peer_select(s1t, s2t, jnp.asarray(cand_pos, F32)[:, None], _pick(n, 512))

    ffn = _peer_experts(hbt, peer_u.astype(BF16), peer_v.T.astype(BF16), ea, n1, eb, r2,
                        _pick(n, 1024), 8 * N_KEYS)

    out = _ln2(h, ffn, ln2_g[None, :], ln2_b[None, :], alpha, _pick(n, 512))
    return out.reshape(bsz, seq, d)


def kernel(x, positions, w_in, conv_w, conv_b, dt_bias, a_log, d_skip, ssm_norm_w, w_out, ln1_g, ln1_b,
           peer_wq, peer_k1, peer_k2, peer_u, peer_v, ln2_g, ln2_b):
    depth = w_in.shape[0]
    alpha = float((2 * depth) ** 0.25)
    for i in range(depth):
        x = _layer(x, positions, w_in[i], conv_w[i], conv_b[i], dt_bias[i], a_log[i], d_skip[i],
                   ssm_norm_w[i], w_out[i], ln1_g[i], ln1_b[i], peer_wq[i], peer_k1[i], peer_k2[i],
                   peer_u[i], peer_v[i], ln2_g[i], ln2_b[i], alpha)
    return x
```
